```python
import jax, jax.numpy as jnp
from jax import lax
import numpy as np

D_MODEL = 1024
BATCH = 8
SEQ = 2048
DEPTH = 2

GM_GROUPS = 4
GM_DIM = 64
GM_WIDTH = GM_GROUPS * GM_DIM
GM_CHUNK = 128
ML_HEADS = 4
ML_QK = 64
ML_V = 128
ML_QK_W = ML_HEADS * ML_QK
ML_V_W = ML_HEADS * ML_V
ML_CHUNK = 64
GD_HEADS = 4
GD_K = 64
GD_V = 64
GD_K_W = GD_HEADS * GD_K
GD_V_W = GD_HEADS * GD_V
GD_CHUNK = 64
GD_CONV = 5
D_FF = 2816
FFN_CONV = 3
EPS = 1e-6

D_MIX = GM_WIDTH + ML_V_W + GD_V_W
SIZES = [GM_WIDTH, GM_WIDTH,
         ML_QK_W, ML_QK_W, ML_V_W, ML_V_W,
         4 * ML_HEADS,
         2 * GD_K_W + GD_V_W,
         GD_V_W,
         2 * GD_HEADS, 2 * GD_HEADS]
N_IN = sum(SIZES)
SPLIT_IDX = [sum(SIZES[:i + 1]) for i in range(len(SIZES) - 1)]

kernel_name = "hybrid_gmlp_mlstm_gdn_encoder"


def rmsnorm(x, g):
    xf = x.astype(jnp.float32)
    y = xf * lax.rsqrt(jnp.mean(xf * xf, axis=-1, keepdims=True) + EPS)
    return (y * g.astype(jnp.float32)).astype(x.dtype)


def l2norm(x):
    return x * lax.rsqrt(jnp.sum(x * x, axis=-1, keepdims=True) + EPS)


def dwconv_centred(x, w):
    width = w.shape[0]
    pad = (width - 1) // 2
    return lax.conv_general_dilated(
        x, w.astype(x.dtype)[:, None, :], window_strides=(1,),
        padding=[(pad, width - 1 - pad)],
        dimension_numbers=("NWC", "WIO", "NWC"),
        feature_group_count=x.shape[-1])


def to_heads(t, n_heads):
    b, s, _ = t.shape
    return t.reshape(b, s, n_heads, -1).transpose(0, 2, 1, 3).astype(jnp.float32)


def flip_t(a):
    return jnp.flip(a, axis=2)


def gmlp_group(u, v, gm_norm, w_s, b_s):
    b, s, _ = u.shape
    nc = s // GM_CHUNK
    vn = rmsnorm(v.reshape(b, s, GM_GROUPS, GM_DIM), gm_norm)
    vn = vn.reshape(b, nc, GM_CHUNK, GM_GROUPS, GM_DIM)
    sg = jnp.einsum("gpq,bcqge->bcpge", w_s.astype(vn.dtype), vn) + b_s.T.astype(vn.dtype)[:, :, None]
    return u * sg.reshape(b, s, GM_WIDTH)


def mlstm_dir(q, k, v, ig, lf):
    b, h, s, dk = q.shape
    dv = v.shape[-1]
    L = ML_CHUNK
    nc = s // L
    q = q.reshape(b, h, nc, L, dk) * (dk ** -0.5)
    k = k.reshape(b, h, nc, L, dk)
    v = v.reshape(b, h, nc, L, dv)
    ig = ig.reshape(b, h, nc, L)
    bcum = jnp.cumsum(lf.reshape(b, h, nc, L), axis=-1)
    b_last = bcum[..., -1]
    a = b_last[..., None] - bcum + ig
    m_loc = jnp.max(a, axis=-1)
    wa = jnp.exp(a - m_loc[..., None])
    C_loc = jnp.einsum("bhcsk,bhcsv->bhckv", k * wa[..., None], v)
    n_loc = jnp.einsum("bhcs,bhcsk->bhck", wa, k)

    def step(carry, xs):
        C, n, m = carry
        bl, ml, Cl, nl = xs
        m_new = jnp.maximum(bl + m, ml)
        f_old = jnp.exp(bl + m - m_new)
        f_loc = jnp.exp(ml - m_new)
        C_new = f_old[..., None, None] * C + f_loc[..., None, None] * Cl
        n_new = f_old[..., None] * n + f_loc[..., None] * nl
        return (C_new, n_new, m_new), (C, n, m)

    init = (jnp.zeros((b, h, dk, dv), jnp.float32), jnp.zeros((b, h, dk), jnp.float32),
            jnp.zeros((b, h), jnp.float32))
    xs = (jnp.moveaxis(b_last, 2, 0), jnp.moveaxis(m_loc, 2, 0),
          jnp.moveaxis(C_loc, 2, 0), jnp.moveaxis(n_loc, 2, 0))
    _, (C_in, n_in, m_in) = lax.scan(step, init, xs)
    C_in = jnp.moveaxis(C_in, 0, 2)
    n_in = jnp.moveaxis(n_in, 0, 2)
    m_in = jnp.moveaxis(m_in, 0, 2)

    lower = jnp.tril(jnp.ones((L, L), bool))
    D = jnp.where(lower, bcum[..., :, None] - bcum[..., None, :] + ig[..., None, :], -jnp.inf)
    m_inter = bcum + m_in[..., None]
    m_t = jnp.maximum(m_inter, jnp.max(D, axis=-1))
    S = jnp.einsum("bhctk,bhcsk->bhcts", q, k) * jnp.exp(D - m_t[..., None])
    w_inter = jnp.exp(m_inter - m_t)
    num = (w_inter[..., None] * jnp.einsum("bhctk,bhckv->bhctv", q, C_in)
           + jnp.einsum("bhcts,bhcsv->bhctv", S, v))
    den = w_inter * jnp.einsum("bhctk,bhck->bhct", q, n_in) + jnp.sum(S, axis=-1)
    out = num / jnp.maximum(jnp.abs(den), jnp.exp(-m_t))[..., None]
    return out.reshape(b, h, s, dv)


def gdn_dir(q, k, v, g, beta):
    b, h, s, dk = q.shape
    dv = v.shape[-1]
    L = GD_CHUNK
    nc = s // L
    q = q.reshape(b, h, nc, L, dk)
    k = k.reshape(b, h, nc, L, dk)
    v = v.reshape(b, h, nc, L, dv)
    beta = beta.reshape(b, h, nc, L)
    gc = jnp.cumsum(g.reshape(b, h, nc, L), axis=-1)
    k_beta = k * beta[..., None]
    v_beta = v * beta[..., None]
    incl = jnp.tril(jnp.ones((L, L), bool))
    strict = jnp.tril(jnp.ones((L, L), bool), k=-1)
    decay = jnp.exp(jnp.where(incl, gc[..., :, None] - gc[..., None, :], -jnp.inf))
    Lmat = jnp.where(strict, jnp.einsum("bhctk,bhcsk->bhcts", k_beta, k) * decay, 0.0)
    eye = jnp.eye(L, dtype=jnp.float32)
    rhs = jnp.concatenate([v_beta, k_beta * jnp.exp(gc)[..., None]], axis=-1)
    sol = lax.linalg.triangular_solve(Lmat + eye, rhs, left_side=True, lower=True,
                                      unit_diagonal=True)
    u_w = sol[..., :dv]
    w_w = sol[..., dv:]
    attn = jnp.where(incl, jnp.einsum("bhctk,bhcsk->bhcts", q, k) * decay, 0.0)

    def step(S, xs):
        q_c, k_c, u_c, w_c, gc_c, attn_c = xs
        v_new = u_c - jnp.einsum("bhtk,bhkv->bhtv", w_c, S)
        o = (jnp.einsum("bhtk,bhkv->bhtv", q_c * jnp.exp(gc_c)[..., None], S)
             + jnp.einsum("bhts,bhsv->bhtv", attn_c, v_new))
        g_last = gc_c[..., -1]
        S_new = (S * jnp.exp(g_last)[..., None, None]
                 + jnp.einsum("bhsk,bhsv->bhkv", k_c * jnp.exp(g_last[..., None] - gc_c)[..., None], v_new))
        return S_new, o

    xs = tuple(jnp.moveaxis(t, 2, 0) for t in (q, k, u_w, w_w, gc, attn))
    _, o = lax.scan(step, jnp.zeros((b, h, dk, dv), jnp.float32), xs)
    return jnp.moveaxis(o, 0, 2).reshape(b, h, s, dv)


def hybrid_layer(x, norm_mix, w_in, gm_norm, gm_ws, gm_bs, ml_gate_bias, ml_head_norm,
                 gd_conv, gd_A_log, gd_dt_bias, gd_head_norm, w_out,
                 norm_ffn, w_up, ffn_conv, ffn_conv_b, w_down):
    b, s, _ = x.shape
    h = rmsnorm(x, norm_mix)
    proj = h @ w_in
    (a_u, a_v, m_q, m_k, m_v, m_o, m_g, c_qkv, c_z, c_a, c_b) = jnp.split(proj, SPLIT_IDX, axis=-1)

    y_a = gmlp_group(jax.nn.gelu(a_u), jax.nn.gelu(a_v), gm_norm, gm_ws, gm_bs)

    q = to_heads(m_q, ML_HEADS)
    k = to_heads(m_k, ML_HEADS)
    v = to_heads(m_v, ML_HEADS)
    gates = (m_g.astype(jnp.float32) + ml_gate_bias.astype(jnp.float32))
    gates = gates.reshape(b, s, 4, ML_HEADS).transpose(2, 0, 3, 1)
    i_fw, i_bw, f_fw, f_bw = gates[0], gates[1], gates[2], gates[3]
    h_fw = mlstm_dir(q, k, v, i_fw, jax.nn.log_sigmoid(f_fw))
    h_bw = flip_t(mlstm_dir(flip_t(q), flip_t(k), flip_t(v), flip_t(i_bw),
                            flip_t(jax.nn.log_sigmoid(f_bw))))
    hB = (h_fw + h_bw).transpose(0, 2, 1, 3)
    hB = rmsnorm(hB, ml_head_norm) * jax.nn.sigmoid(m_o.astype(jnp.float32)).reshape(b, s, ML_HEADS, ML_V)
    y_b = hB.reshape(b, s, ML_V_W)

    qkv = jax.nn.silu(dwconv_centred(c_qkv, gd_conv))
    cq, ck, cv = jnp.split(qkv, [GD_K_W, 2 * GD_K_W], axis=-1)
    gq = l2norm(to_heads(cq, GD_HEADS)) * (GD_K ** -0.5)
    gk = l2norm(to_heads(ck, GD_HEADS))
    gv = to_heads(cv, GD_HEADS)
    a_in = c_a.astype(jnp.float32).reshape(b, s, 2, GD_HEADS).transpose(2, 0, 3, 1)
    beta = jax.nn.sigmoid(c_b.astype(jnp.float32).reshape(b, s, 2, GD_HEADS).transpose(2, 0, 3, 1))
    A = jnp.exp(gd_A_log.astype(jnp.float32))[:, None, :, None]
    g = -A * jax.nn.softplus(a_in + gd_dt_bias.astype(jnp.float32)[:, None, :, None])
    o_fw = gdn_dir(gq, gk, gv, g[0], beta[0])
    o_bw = flip_t(gdn_dir(flip_t(gq), flip_t(gk), flip_t(gv), flip_t(g[1]), flip_t(beta[1])))
    oC = (o_fw + o_bw).transpose(0, 2, 1, 3)
    oC = rmsnorm(oC, gd_head_norm) * jax.nn.silu(c_z.astype(jnp.float32)).reshape(b, s, GD_HEADS, GD_V)
    y_c = oC.reshape(b, s, GD_V_W)

    y = jnp.concatenate([y_a.astype(x.dtype), y_b.astype(x.dtype), y_c.astype(x.dtype)], axis=-1)
    x = x + y @ w_out

    hf = rmsnorm(x, norm_ffn)
    up = dwconv_centred(hf @ w_up, ffn_conv) + ffn_conv_b
    gt, val = jnp.split(up, 2, axis=-1)
    return x + (jax.nn.silu(gt) * val) @ w_down


def setup_inputs(seed: int = 0) -> dict:
    key = jax.random.key(seed)
    ks = jax.random.split(key, 24)

    def nrm(k, shape, scale):
        return jax.random.normal(k, shape, jnp.float32) * scale

    def gain(k, shape):
        return 1.0 + 0.02 * jax.random.normal(k, shape, jnp.float32)

    x = jax.random.normal(ks[0], (BATCH, SEQ, D_MODEL), jnp.float32)
    norm_mix = gain(ks[1], (DEPTH, D_MODEL))
    w_in = nrm(ks[2], (DEPTH, D_MODEL, N_IN), D_MODEL ** -0.5)
    gm_norm = gain(ks[3], (DEPTH, GM_GROUPS, GM_DIM))
    gm_ws = nrm(ks[4], (DEPTH, GM_GROUPS, GM_CHUNK, GM_CHUNK), GM_CHUNK ** -0.5)
    gm_bs = 1.0 + 0.1 * jax.random.normal(ks[5], (DEPTH, GM_GROUPS, GM_CHUNK), jnp.float32)
    i_bias = 0.1 * jax.random.normal(ks[6], (DEPTH, 2 * ML_HEADS), jnp.float32)
    f_bias = 3.0 + 0.5 * jax.random.normal(ks[7], (DEPTH, 2 * ML_HEADS), jnp.float32)
    ml_gate_bias = jnp.concatenate([i_bias, f_bias], axis=-1)
    ml_head_norm = gain(ks[8], (DEPTH, ML_HEADS, ML_V))
    gd_conv = nrm(ks[9], (DEPTH, GD_CONV, 2 * GD_K_W + GD_V_W), GD_CONV ** -0.5)
    gd_A_log = jnp.log(jax.random.uniform(ks[10], (DEPTH, 2, GD_HEADS), jnp.float32, 1.0, 16.0))
    dt = jnp.exp(jax.random.uniform(ks[11], (DEPTH, 2, GD_HEADS), jnp.float32,
                                    float(np.log(1e-3)), float(np.log(1e-1))))
    gd_dt_bias = dt + jnp.log(-jnp.expm1(-dt))
    gd_head_norm = gain(ks[12], (DEPTH, GD_HEADS, GD_V))
    w_out = nrm(ks[13], (DEPTH, D_MIX, D_MODEL), D_MIX ** -0.5)
    norm_ffn = gain(ks[14], (DEPTH, D_MODEL))
    w_up = nrm(ks[15], (DEPTH, D_MODEL, 2 * D_FF), D_MODEL ** -0.5)
    ffn_conv = nrm(ks[16], (DEPTH, FFN_CONV, 2 * D_FF), FFN_CONV ** -0.5)
    ffn_conv_b = nrm(ks[17], (DEPTH, 2 * D_FF), 0.02)
    w_down = nrm(ks[18], (DEPTH, D_FF, D_MODEL), D_FF ** -0.5)
    norm_final = gain(ks[19], (D_MODEL,))
    return {"x": x, "norm_mix": norm_mix, "w_in": w_in, "gm_norm": gm_norm, "gm_ws": gm_ws,
            "gm_bs": gm_bs, "ml_gate_bias": ml_gate_bias, "ml_head_norm": ml_head_norm,
            "gd_conv": gd_conv, "gd_A_log": gd_A_log, "gd_dt_bias": gd_dt_bias,
            "gd_head_norm": gd_head_norm, "w_out": w_out, "norm_ffn": norm_ffn, "w_up": w_up,
            "ffn_conv": ffn_conv, "ffn_conv_b": ffn_conv_b, "w_down": w_down,
            "norm_final": norm_final}


def reference(x, norm_mix, w_in, gm_norm, gm_ws, gm_bs, ml_gate_bias, ml_head_norm,
              gd_conv, gd_A_log, gd_dt_bias, gd_head_norm, w_out, norm_ffn, w_up,
              ffn_conv, ffn_conv_b, w_down, norm_final):
    for i in range(DEPTH):
        x = hybrid_layer(x, norm_mix[i], w_in[i], gm_norm[i], gm_ws[i], gm_bs[i],
                         ml_gate_bias[i], ml_head_norm[i], gd_conv[i], gd_A_log[i],
                         gd_dt_bias[i], gd_head_norm[i], w_out[i], norm_ffn[i], w_up[i],
                         ffn_conv[i], ffn_conv_b[i], w_down[i])
    return rmsnorm(x, norm_final)
```

```python
import functools

import jax
import jax.numpy as jnp
from jax import lax
from jax.experimental import pallas as pl
from jax.experimental.pallas import tpu as pltpu

D_MODEL = 1024
GM_GROUPS = 4
GM_DIM = 64
GM_WIDTH = GM_GROUPS * GM_DIM
GM_CHUNK = 128
ML_HEADS = 4
ML_QK = 64
ML_V = 128
ML_QK_W = ML_HEADS * ML_QK
ML_V_W = ML_HEADS * ML_V
ML_CHUNK = 64
GD_HEADS = 4
GD_K = 64
GD_V = 64
GD_K_W = GD_HEADS * GD_K
GD_V_W = GD_HEADS * GD_V
GD_CHUNK = 64
GD_CONV = 5
D_FF = 2816
FFN_CONV = 3
EPS = 1e-6

LANES = 128
N_GATES = 32
GATE_I, GATE_F, GATE_A, GATE_B = 0, 8, 16, 24
VMEM_LIMIT = 56 * 1024 * 1024

BF = jnp.bfloat16
F32 = jnp.float32


def _dot(a, b):
    return jnp.dot(a, b, preferred_element_type=F32)


def _dot_nt(a, b):
    return lax.dot_general(a, b, (((1,), (1,)), ((), ())), preferred_element_type=F32)


def _dot_tn(a, b):
    return lax.dot_general(a, b, (((0,), (0,)), ((), ())), preferred_element_type=F32)


def _split3(x):
    hi = x.astype(BF)
    r1 = x - hi.astype(F32)
    mid = r1.astype(BF)
    lo = (r1 - mid.astype(F32)).astype(BF)
    return hi, mid, lo


def _sel_dot_rhs(sel, x):
    hi, mid, lo = _split3(x)
    return _dot(sel, hi) + _dot(sel, mid) + _dot(sel, lo)


def _sel_dot_lhs(x, sel):
    hi, mid, lo = _split3(x)
    return _dot(hi, sel) + _dot(mid, sel) + _dot(lo, sel)


def _softplus(x):
    return jnp.maximum(x, 0.0) + jnp.log1p(jnp.exp(-jnp.abs(x)))


def _log_sigmoid(x):
    return jnp.minimum(x, 0.0) - jnp.log1p(jnp.exp(-jnp.abs(x)))


def _tri_masks(n):
    r = lax.broadcasted_iota(jnp.int32, (n, n), 0)
    c = lax.broadcasted_iota(jnp.int32, (n, n), 1)
    return c <= r, c >= r, c < r, c > r


def _group_ones(width, group):
    r = lax.broadcasted_iota(jnp.int32, (width, width), 0) // group
    c = lax.broadcasted_iota(jnp.int32, (width, width), 1) // group
    return jnp.where(r == c, 1.0, 0.0).astype(BF)


_IN_SEGS = ((0, 512), (512, 1024), (1024, 1536), (1536, 2048), (2048, 2816), (2816, 3072))
_IN_MAIN = 3072


def _inproj_kernel(x_ref, g_ref, w_ref, wg_ref, b_ref, o_uv, o_qk, o_v, o_o, o_c, o_z, o_g):
    x = x_ref[...]
    h = (x * lax.rsqrt(jnp.mean(x * x, axis=-1, keepdims=True) + EPS) * g_ref[...]).astype(BF)
    for o, (lo, hi) in zip((o_uv, o_qk, o_v, o_o, o_c, o_z), _IN_SEGS):
        o[...] = _dot(h, w_ref[:, lo:hi]).astype(o.dtype)
    o_g[...] = _dot(h, wg_ref[...]) + b_ref[...]


def _inproj(x2, gain, w_main, w_gate, gate_bias, tm):
    n = x2.shape[0]
    widths = [hi - lo for lo, hi in _IN_SEGS]
    out_shape = [jax.ShapeDtypeStruct((n, w), BF) for w in widths]
    out_shape.append(jax.ShapeDtypeStruct((n, LANES), F32))
    row = lambda i: (i, 0)
    const = lambda i: (0, 0)
    return pl.pallas_call(
        _inproj_kernel,
        out_shape=out_shape,
        grid=(n // tm,),
        in_specs=[
            pl.BlockSpec((tm, D_MODEL), row),
            pl.BlockSpec((1, D_MODEL), const),
            pl.BlockSpec((D_MODEL, _IN_MAIN), const),
            pl.BlockSpec((D_MODEL, LANES), const),
            pl.BlockSpec((1, LANES), const),
        ],
        out_specs=[pl.BlockSpec((tm, w), row) for w in widths] + [pl.BlockSpec((tm, LANES), row)],
        compiler_params=pltpu.CompilerParams(
            dimension_semantics=("parallel",), vmem_limit_bytes=VMEM_LIMIT),
    )(x2, gain, w_main, w_gate, gate_bias)


def _gmlp_kernel(uv_ref, gn_ref, ws_ref, bs_ref, y_ref):
    t = uv_ref.shape[0]
    gsum = _group_ones(GM_WIDTH, GM_DIM)
    rr = lax.broadcasted_iota(jnp.int32, (GM_GROUPS * GM_CHUNK, GM_WIDTH), 0) // GM_CHUNK
    cc = lax.broadcasted_iota(jnp.int32, (GM_GROUPS * GM_CHUNK, GM_WIDTH), 1) // GM_DIM
    blockdiag = rr == cc
    ws = ws_ref[...]
    bs = bs_ref[...]
    gn = gn_ref[...]

    def body(c, carry):
        off = pl.multiple_of(c * GM_CHUNK, GM_CHUNK)
        uv = uv_ref[pl.ds(off, GM_CHUNK), :].astype(F32)
        u = jax.nn.gelu(uv[:, :GM_WIDTH])
        v = jax.nn.gelu(uv[:, GM_WIDTH:])
        ms = _sel_dot_lhs(v * v, gsum) * (1.0 / GM_DIM)
        vn = (v * lax.rsqrt(ms + EPS) * gn).astype(BF)
        vexp = jnp.where(blockdiag, jnp.concatenate([vn] * GM_GROUPS, axis=0), jnp.zeros((), BF))
        sg = _dot(ws, vexp) + bs
        y_ref[pl.ds(off, GM_CHUNK), :] = (u * sg).astype(y_ref.dtype)
        return carry

    lax.fori_loop(0, t // GM_CHUNK, body, 0)


def _gmlp(uv, gn, ws_cat, bs_exp):
    b, t, _ = uv.shape
    const = lambda i: (0, 0)
    return pl.pallas_call(
        _gmlp_kernel,
        out_shape=jax.ShapeDtypeStruct((b, t, GM_WIDTH), BF),
        grid=(b,),
        in_specs=[
            pl.BlockSpec((None, t, 2 * GM_WIDTH), lambda i: (i, 0, 0)),
            pl.BlockSpec((1, GM_WIDTH), const),
            pl.BlockSpec((GM_CHUNK, GM_GROUPS * GM_CHUNK), const),
            pl.BlockSpec((GM_CHUNK, GM_WIDTH), const),
        ],
        out_specs=pl.BlockSpec((None, t, GM_WIDTH), lambda i: (i, 0, 0)),
        compiler_params=pltpu.CompilerParams(
            dimension_semantics=("parallel",), vmem_limit_bytes=VMEM_LIMIT),
    )(uv, gn, ws_cat, bs_exp)


def _mlstm_kernel(qk_ref, v_ref, og_ref, gc_ref, gr_ref, hn_ref, y_ref, acc_ref):
    t = qk_ref.shape[0]
    L = ML_CHUNK
    nc = t // L
    lower, upper, _, _ = _tri_masks(L)
    lower_b = jnp.where(lower, 1.0, 0.0).astype(BF)
    upper_b = jnp.where(upper, 1.0, 0.0).astype(BF)
    ones_v = jnp.ones((L, ML_V), BF)
    acc_ref[...] = jnp.zeros_like(acc_ref)

    def chunk_dir(cidx, d, state):
        off = pl.multiple_of(cidx * L, L)
        qkc = qk_ref[pl.ds(off, L), :]
        vc = v_ref[pl.ds(off, L), :]
        gcol = gc_ref[pl.ds(off, L), :]
        grow = gr_ref[cidx]
        mask = lower if d == 0 else upper
        bc_col = _sel_dot_rhs(lower_b if d == 0 else upper_b, _log_sigmoid(gcol))
        bc_row = _sel_dot_lhs(_log_sigmoid(grow), upper_b if d == 0 else lower_b)
        new_state = []
        for h in range(ML_HEADS):
            c_aug, m = state[h]
            q = qkc[:, h * ML_QK:(h + 1) * ML_QK]
            k = qkc[:, ML_QK_W + h * ML_QK:ML_QK_W + (h + 1) * ML_QK]
            v_aug = jnp.concatenate([vc[:, h * ML_V:(h + 1) * ML_V], ones_v], axis=1)
            qs = (q.astype(F32) * (ML_QK ** -0.5)).astype(BF)
            qk = _dot_nt(qs, k)
            ci = GATE_I + ML_HEADS * d + h
            cf = GATE_F + ML_HEADS * d + h
            ig_r = grow[ci:ci + 1, :]
            ig_c = gcol[:, ci:ci + 1]
            bc_r = bc_row[cf:cf + 1, :]
            bc_c = bc_col[:, cf:cf + 1]
            b_last = bc_r[:, L - 1:L] if d == 0 else bc_r[:, 0:1]
            a_c = b_last - bc_c + ig_c
            m_loc = jnp.max(a_c, axis=0, keepdims=True)
            kw = (k.astype(F32) * jnp.exp(a_c - m_loc)).astype(BF)
            c_loc = _dot_tn(kw, v_aug)
            m_new = jnp.maximum(b_last + m, m_loc)
            c_new = jnp.exp(b_last + m - m_new) * c_aug + jnp.exp(m_loc - m_new) * c_loc
            new_state.append((c_new, m_new))
            dmat = jnp.where(mask, bc_c - bc_r + ig_r, -jnp.inf)
            m_inter = bc_c + m
            m_t = jnp.maximum(m_inter, jnp.max(dmat, axis=1, keepdims=True))
            s = qk * jnp.exp(dmat - m_t)
            na = jnp.exp(m_inter - m_t) * _dot(qs, c_aug.astype(BF)) + _dot(s.astype(BF), v_aug)
            out = na[:, :ML_V] / jnp.maximum(jnp.abs(na[:, ML_V:]), jnp.exp(-m_t))
            acc_ref[pl.ds(off, L), h * ML_V:(h + 1) * ML_V] += out
        return tuple(new_state)

    def body(i, carry):
        fw, bw = carry
        return chunk_dir(i, 0, fw), chunk_dir(nc - 1 - i, 1, bw)

    init = tuple((jnp.zeros((ML_QK, 2 * ML_V), F32), jnp.zeros((1, 1), F32)) for _ in range(ML_HEADS))
    lax.fori_loop(0, nc, body, (init, init))

    hn = hn_ref[...]
    rows = 256

    def epilogue(j, carry):
        off = pl.multiple_of(j * rows, rows)
        hb = acc_ref[pl.ds(off, rows), :]
        gate = jax.nn.sigmoid(og_ref[pl.ds(off, rows), :].astype(F32))
        for h in range(ML_HEADS):
            sl = slice(h * ML_V, (h + 1) * ML_V)
            x = hb[:, sl]
            y = x * lax.rsqrt(jnp.mean(x * x, axis=-1, keepdims=True) + EPS) * hn[:, sl]
            y_ref[pl.ds(off, rows), sl] = (y * gate[:, sl]).astype(y_ref.dtype)
        return carry

    lax.fori_loop(0, t // rows, epilogue, 0)


def _mlstm(qk, v, og, g_col, g_row, hn):
    b, t, _ = qk.shape
    nc = t // ML_CHUNK
    seq = lambda i: (i, 0, 0)
    return pl.pallas_call(
        _mlstm_kernel,
        out_shape=jax.ShapeDtypeStruct((b, t, ML_V_W), BF),
        grid=(b,),
        in_specs=[
            pl.BlockSpec((None, t, 2 * ML_QK_W), seq),
            pl.BlockSpec((None, t, ML_V_W), seq),
            pl.BlockSpec((None, t, ML_V_W), seq),
            pl.BlockSpec((None, t, LANES), seq),
            pl.BlockSpec((None, nc, N_GATES, ML_CHUNK), lambda i: (i, 0, 0, 0)),
            pl.BlockSpec((1, ML_V_W), lambda i: (0, 0)),
        ],
        out_specs=pl.BlockSpec((None, t, ML_V_W), seq),
        scratch_shapes=[pltpu.VMEM((t, ML_V_W), F32)],
        compiler_params=pltpu.CompilerParams(
            dimension_semantics=("parallel",), vmem_limit_bytes=VMEM_LIMIT),
    )(qk, v, og, g_col, g_row, hn)


_GD_QKV_W = 2 * GD_K_W + GD_V_W
_GD_HALO = 8


def _unit_lower_solve(lmat, rhs):
    n = lmat.shape[0]
    p = lmat.astype(BF)
    x = rhs - _dot(p, rhs.astype(BF))
    power = 2
    while power < n:
        p32 = _dot(p, p)
        p = p32.astype(BF)
        x = x + _dot(p, x.astype(BF))
        power *= 2
    return x


def _gdn_kernel(c_ref, z_ref, gc_ref, gr_ref, cw_ref, acol_ref, arow_ref, hn_ref, y_ref,
                xp_ref, qkv_ref, acc_ref):
    t = c_ref.shape[0]
    L = GD_CHUNK
    nc = t // L
    incl_lo, incl_up, strict_lo, strict_up = _tri_masks(L)
    lower_b = jnp.where(incl_lo, 1.0, 0.0).astype(BF)
    upper_b = jnp.where(incl_up, 1.0, 0.0).astype(BF)
    gsum = _group_ones(LANES, GD_K)

    xp_ref[0:_GD_HALO, :] = jnp.zeros((_GD_HALO, _GD_QKV_W), F32)
    xp_ref[t + _GD_HALO:t + 2 * _GD_HALO, :] = jnp.zeros((_GD_HALO, _GD_QKV_W), F32)
    xp_ref[_GD_HALO:t + _GD_HALO, :] = c_ref[...].astype(F32)
    cw = cw_ref[...]
    rows = 256
    pad = (GD_CONV - 1) // 2

    def conv_tile(j, carry):
        off = pl.multiple_of(j * rows, rows)
        xt = xp_ref[pl.ds(off, rows + 2 * _GD_HALO), :]
        y = jnp.zeros((rows, _GD_QKV_W), F32)
        for tap in range(GD_CONV):
            s0 = _GD_HALO + tap - pad
            y = y + cw[tap:tap + 1, :] * xt[s0:s0 + rows, :]
        y = y * jax.nn.sigmoid(y)
        for s in range(2 * GD_K_W // LANES):
            sl = slice(s * LANES, (s + 1) * LANES)
            ys = y[:, sl]
            inv = lax.rsqrt(_sel_dot_lhs(ys * ys, gsum) + EPS)
            scale = GD_K ** -0.5 if s < GD_K_W // LANES else 1.0
            qkv_ref[pl.ds(off, rows), sl] = ys * inv * scale
        qkv_ref[pl.ds(off, rows), 2 * GD_K_W:] = y[:, 2 * GD_K_W:]
        return carry

    lax.fori_loop(0, t // rows, conv_tile, 0)

    acc_ref[...] = jnp.zeros_like(acc_ref)
    a_col = jnp.exp(acol_ref[...])
    a_row = jnp.exp(arow_ref[...])

    def chunk_dir(cidx, d, state):
        off = pl.multiple_of(cidx * L, L)
        gcol = gc_ref[pl.ds(off, L), :]
        grow = gr_ref[cidx]
        qkv = qkv_ref[pl.ds(off, L), :]
        incl = incl_lo if d == 0 else incl_up
        strict = strict_lo if d == 0 else strict_up
        g_col = -a_col * _softplus(gcol)
        g_row = -a_row * _softplus(grow)
        gcum_col = _sel_dot_rhs(lower_b if d == 0 else upper_b, g_col)
        gcum_row = _sel_dot_lhs(g_row, upper_b if d == 0 else lower_b)
        beta_col = jax.nn.sigmoid(gcol)
        new_state = []
        for h in range(GD_HEADS):
            s_in = state[h]
            ca = GATE_A + GD_HEADS * d + h
            cb = GATE_B + GD_HEADS * d + h
            gc_c = gcum_col[:, ca:ca + 1]
            gc_r = gcum_row[ca:ca + 1, :]
            be_c = beta_col[:, cb:cb + 1]
            q = qkv[:, h * GD_K:(h + 1) * GD_K]
            k = qkv[:, GD_K_W + h * GD_K:GD_K_W + (h + 1) * GD_K]
            v = qkv[:, 2 * GD_K_W + h * GD_V:2 * GD_K_W + (h + 1) * GD_V]
            kb = k.astype(BF)
            kk = _dot_nt(kb, kb)
            qk = _dot_nt(q.astype(BF), kb)
            decay = jnp.exp(jnp.where(incl, gc_c - gc_r, -jnp.inf))
            lmat = jnp.where(strict, be_c * kk * decay, 0.0)
            eg = jnp.exp(gc_c)
            rhs = jnp.concatenate([v * be_c, k * (be_c * eg)], axis=1)
            sol = _unit_lower_solve(lmat, rhs)
            u_w = sol[:, :GD_V]
            w_w = sol[:, GD_V:]
            s_b = s_in.astype(BF)
            v_new = u_w - _dot(w_w.astype(BF), s_b)
            v_new_b = v_new.astype(BF)
            o = _dot((q * eg).astype(BF), s_b) + _dot((qk * decay).astype(BF), v_new_b)
            g_last = gc_r[:, L - 1:L] if d == 0 else gc_r[:, 0:1]
            kd = (k * jnp.exp(g_last - gc_c)).astype(BF)
            new_state.append(s_in * jnp.exp(g_last) + _dot_tn(kd, v_new_b))
            acc_ref[pl.ds(off, L), h * GD_V:(h + 1) * GD_V] += o
        return tuple(new_state)

    def body(i, carry):
        fw, bw = carry
        return chunk_dir(i, 0, fw), chunk_dir(nc - 1 - i, 1, bw)

    init = tuple(jnp.zeros((GD_K, GD_V), F32) for _ in range(GD_HEADS))
    lax.fori_loop(0, nc, body, (init, init))

    hn = hn_ref[...]
    gsum_v = _group_ones(GD_V_W, GD_V)

    def epilogue(j, carry):
        off = pl.multiple_of(j * rows, rows)
        x = acc_ref[pl.ds(off, rows), :]
        z = z_ref[pl.ds(off, rows), :].astype(F32)
        ms = _sel_dot_lhs(x * x, gsum_v) * (1.0 / GD_V)
        y = x * lax.rsqrt(ms + EPS) * hn
        y_ref[pl.ds(off, rows), :] = (y * (z * jax.nn.sigmoid(z))).astype(y_ref.dtype)
        return carry

    lax.fori_loop(0, t // rows, epilogue, 0)


def _gdn(c_qkv, z, g_col, g_row, conv_w, a_col, a_row, hn):
    b, t, _ = c_qkv.shape
    nc = t // GD_CHUNK
    seq = lambda i: (i, 0, 0)
    const = lambda i: (0, 0)
    return pl.pallas_call(
        _gdn_kernel,
        out_shape=jax.ShapeDtypeStruct((b, t, GD_V_W), BF),
        grid=(b,),
        in_specs=[
            pl.BlockSpec((None, t, _GD_QKV_W), seq),
            pl.BlockSpec((None, t, GD_V_W), seq),
            pl.BlockSpec((None, t, LANES), seq),
            pl.BlockSpec((None, nc, N_GATES, GD_CHUNK), lambda i: (i, 0, 0, 0)),
            pl.BlockSpec((GD_CONV, _GD_QKV_W), const),
            pl.BlockSpec((1, LANES), const),
            pl.BlockSpec((N_GATES, 1), const),
            pl.BlockSpec((1, GD_V_W), const),
        ],
        out_specs=pl.BlockSpec((None, t, GD_V_W), seq),
        scratch_shapes=[
            pltpu.VMEM((t + 2 * _GD_HALO, _GD_QKV_W), F32),
            pltpu.VMEM((t, _GD_QKV_W), F32),
            pltpu.VMEM((t, GD_V_W), F32),
        ],
        compiler_params=pltpu.CompilerParams(
            dimension_semantics=("parallel",), vmem_limit_bytes=VMEM_LIMIT),
    )(c_qkv, z, g_col, g_row, conv_w, a_col, a_row, hn)


def _outproj_kernel(x_ref, ya_ref, yb_ref, yc_ref, w_ref, o_ref):
    acc = x_ref[...]
    acc = acc + _dot(ya_ref[...], w_ref[0:GM_WIDTH, :])
    acc = acc + _dot(yb_ref[...], w_ref[GM_WIDTH:GM_WIDTH + ML_V_W, :])
    acc = acc + _dot(yc_ref[...], w_ref[GM_WIDTH + ML_V_W:, :])
    o_ref[...] = acc


def _outproj(x2, ya, yb, yc, w, tm):
    n = x2.shape[0]
    row = lambda i: (i, 0)
    return pl.pallas_call(
        _outproj_kernel,
        out_shape=jax.ShapeDtypeStruct((n, D_MODEL), F32),
        grid=(n // tm,),
        in_specs=[
            pl.BlockSpec((tm, D_MODEL), row),
            pl.BlockSpec((tm, GM_WIDTH), row),
            pl.BlockSpec((tm, ML_V_W), row),
            pl.BlockSpec((tm, GD_V_W), row),
            pl.BlockSpec((D_MODEL, D_MODEL), lambda i: (0, 0)),
        ],
        out_specs=pl.BlockSpec((tm, D_MODEL), row),
        compiler_params=pltpu.CompilerParams(
            dimension_semantics=("parallel",), vmem_limit_bytes=VMEM_LIMIT),
    )(x2, ya, yb, yc, w)


_FFN_HALO = 16
_FFN_CHUNK = 256


def _ffn_kernel(xm_ref, xp_ref, xn_ref, g_ref, wup_ref, cw_ref, cb_ref, wdn_ref, gf_ref, o_ref,
                h_ref, acc_ref, *, tiles_per_seq, final_norm):
    i = pl.program_id(0)
    tm = xm_ref.shape[0]
    g = g_ref[...]
    pos = i % tiles_per_seq

    def norm(x):
        return x * lax.rsqrt(jnp.mean(x * x, axis=-1, keepdims=True) + EPS) * g

    hp = jnp.where(pos == 0, 0.0, norm(xp_ref[...]))
    hn = jnp.where(pos == tiles_per_seq - 1, 0.0, norm(xn_ref[...]))
    xm = xm_ref[...]
    h_ref[0:_FFN_HALO, :] = hp.astype(BF)
    h_ref[_FFN_HALO:_FFN_HALO + tm, :] = norm(xm).astype(BF)
    h_ref[_FFN_HALO + tm:, :] = hn.astype(BF)
    hh = h_ref[...]
    cw = cw_ref[...]
    cb = cb_ref[...]
    acc_ref[...] = xm

    def conv(u, c0):
        sl = slice(c0, c0 + _FFN_CHUNK)
        return (cw[0:1, sl] * u[_FFN_HALO - 1:_FFN_HALO - 1 + tm, :]
                + cw[1:2, sl] * u[_FFN_HALO:_FFN_HALO + tm, :]
                + cw[2:3, sl] * u[_FFN_HALO + 1:_FFN_HALO + 1 + tm, :] + cb[:, sl])

    for j in range(D_FF // _FFN_CHUNK):
        c0 = j * _FFN_CHUNK
        gt = conv(_dot(hh, wup_ref[:, c0:c0 + _FFN_CHUNK]), c0)
        vl = conv(_dot(hh, wup_ref[:, D_FF + c0:D_FF + c0 + _FFN_CHUNK]), D_FF + c0)
        act = (gt * jax.nn.sigmoid(gt) * vl).astype(BF)
        acc_ref[...] += _dot(act, wdn_ref[c0:c0 + _FFN_CHUNK, :])

    y = acc_ref[...]
    if final_norm:
        y = y * lax.rsqrt(jnp.mean(y * y, axis=-1, keepdims=True) + EPS) * gf_ref[...]
    o_ref[...] = y


def _ffn(x2, gain, w_up, conv_w, conv_b, w_down, gain_final, seq_len, tm, final_norm):
    n = x2.shape[0]
    hb = tm // _FFN_HALO
    nhb = n // _FFN_HALO
    row = lambda i: (i, 0)
    const = lambda i: (0, 0)
    kern = functools.partial(_ffn_kernel, tiles_per_seq=seq_len // tm, final_norm=final_norm)
    return pl.pallas_call(
        kern,
        out_shape=jax.ShapeDtypeStruct((n, D_MODEL), F32),
        grid=(n // tm,),
        in_specs=[
            pl.BlockSpec((tm, D_MODEL), row),
            pl.BlockSpec((_FFN_HALO, D_MODEL), lambda i: (jnp.maximum(i * hb - 1, 0), 0)),
            pl.BlockSpec((_FFN_HALO, D_MODEL), lambda i: (jnp.minimum((i + 1) * hb, nhb - 1), 0)),
            pl.BlockSpec((1, D_MODEL), const),
            pl.BlockSpec((D_MODEL, 2 * D_FF), const, pipeline_mode=pl.Buffered(1)),
            pl.BlockSpec((FFN_CONV, 2 * D_FF), const),
            pl.BlockSpec((1, 2 * D_FF), const),
            pl.BlockSpec((D_FF, D_MODEL), const, pipeline_mode=pl.Buffered(1)),
            pl.BlockSpec((1, D_MODEL), const),
        ],
        out_specs=pl.BlockSpec((tm, D_MODEL), row),
        scratch_shapes=[
            pltpu.VMEM((tm + 2 * _FFN_HALO, D_MODEL), BF),
            pltpu.VMEM((tm, D_MODEL), F32),
        ],
        compiler_params=pltpu.CompilerParams(
            dimension_semantics=("parallel",), vmem_limit_bytes=VMEM_LIMIT),
    )(x2, x2, x2, gain, w_up, conv_w, conv_b, w_down, gain_final)


_TM = 512


def _layer(x2, batch, seq_len, norm_mix, w_in, gm_norm, gm_ws, gm_bs, ml_gate_bias, ml_head_norm,
           gd_conv, gd_a_log, gd_dt_bias, gd_head_norm, w_out, norm_ffn, w_up, ffn_conv, ffn_conv_b,
           w_down, norm_final, final_norm):
    n = x2.shape[0]
    g0 = 2 * GM_WIDTH + 2 * ML_QK_W + 2 * ML_V_W
    g1 = g0 + 4 * ML_HEADS
    c1 = g1 + _GD_QKV_W + GD_V_W
    w_main = jnp.concatenate([w_in[:, :g0], w_in[:, g1:c1]], axis=1).astype(BF)
    w_gate = jnp.concatenate(
        [w_in[:, g0:g1], w_in[:, c1:], jnp.zeros((D_MODEL, LANES - N_GATES), F32)], axis=1).astype(BF)
    gate_bias = jnp.concatenate(
        [ml_gate_bias, gd_dt_bias.reshape(-1), jnp.zeros((LANES - GATE_B,), F32)]).reshape(1, LANES)
    uv, qk, v, og, cq, z, gates = _inproj(x2, norm_mix.reshape(1, D_MODEL), w_main, w_gate, gate_bias, _TM)

    seq = lambda a: a.reshape(batch, seq_len, a.shape[-1])
    g_col = seq(gates)
    g_row = g_col[:, :, :N_GATES].reshape(batch, seq_len // ML_CHUNK, ML_CHUNK, N_GATES).transpose(0, 1, 3, 2)

    ws_cat = gm_ws.transpose(1, 0, 2).reshape(GM_CHUNK, GM_GROUPS * GM_CHUNK).astype(BF)
    bs_exp = jnp.repeat(gm_bs.T, GM_DIM, axis=1)
    ya = _gmlp(seq(uv), gm_norm.reshape(1, GM_WIDTH), ws_cat, bs_exp)

    yb = _mlstm(seq(qk), seq(v), seq(og), g_col, g_row, ml_head_norm.reshape(1, ML_V_W))

    a_flat = gd_a_log.reshape(-1)
    a_col = jnp.zeros((LANES,), F32).at[GATE_A:GATE_B].set(a_flat).reshape(1, LANES)
    a_row = jnp.zeros((N_GATES,), F32).at[GATE_A:GATE_B].set(a_flat).reshape(N_GATES, 1)
    yc = _gdn(seq(cq), seq(z), g_col, g_row, gd_conv, a_col, a_row, gd_head_norm.reshape(1, GD_V_W))

    flat = lambda a: a.reshape(n, a.shape[-1])
    x2 = _outproj(x2, flat(ya), flat(yb), flat(yc), w_out.astype(BF), _TM)
    return _ffn(x2, norm_ffn.reshape(1, D_MODEL), w_up.astype(BF), ffn_conv, ffn_conv_b.reshape(1, -1),
                w_down.astype(BF), norm_final.reshape(1, D_MODEL), seq_len, _TM, final_norm)


def kernel(x, norm_mix, w_in, gm_norm, gm_ws, gm_bs, ml_gate_bias, ml_head_norm, gd_conv, gd_A_log,
           gd_dt_bias, gd_head_norm, w_out, norm_ffn, w_up, ffn_conv, ffn_conv_b, w_down, norm_final):
    batch, seq_len, _ = x.shape
    depth = w_in.shape[0]
    x2 = x.reshape(batch * seq_len, D_MODEL)
    for i in range(depth):
        x2 = _layer(x2, batch, seq_len, norm_mix[i], w_in[i], gm_norm[i], gm_ws[i], gm_bs[i],
                    ml_gate_bias[i], ml_head_norm[i], gd_conv[i], gd_A_log[i], gd_dt_bias[i],
                    gd_head_norm[i], w_out[i], norm_ffn[i], w_up[i], ffn_conv[i], ffn_conv_b[i],
                    w_down[i], norm_final, i == depth - 1)
    return x2.reshape(batch, seq_len, D_MODEL)
```

```python
import functools

import jax
import jax.numpy as jnp
from jax import lax
from jax.experimental import pallas as pl
from jax.experimental.pallas import tpu as pltpu

D_MODEL = 1024
GM_GROUPS = 4
GM_DIM = 64
GM_WIDTH = GM_GROUPS * GM_DIM
GM_CHUNK = 128
ML_HEADS = 4
ML_QK = 64
ML_V = 128
ML_QK_W = ML_HEADS * ML_QK
ML_V_W = ML_HEADS * ML_V
ML_CHUNK = 64
GD_HEADS = 4
GD_K = 64
GD_V = 64
GD_K_W = GD_HEADS * GD_K
GD_V_W = GD_HEADS * GD_V
GD_CHUNK = 64
GD_CONV = 5
D_FF = 2816
FFN_CONV = 3
EPS = 1e-6

LANES = 128
N_GATES = 32
GATE_I, GATE_F, GATE_A, GATE_B = 0, 8, 16, 24
VMEM_LIMIT = 56 * 1024 * 1024

BF = jnp.bfloat16
F32 = jnp.float32


def _dot(a, b):
    return jnp.dot(a, b, preferred_element_type=F32)


def _dot_nt(a, b):
    return lax.dot_general(a, b, (((1,), (1,)), ((), ())), preferred_element_type=F32)


def _dot_tn(a, b):
    return lax.dot_general(a, b, (((0,), (0,)), ((), ())), preferred_element_type=F32)


def _split3(x):
    hi = x.astype(BF)
    r1 = x - hi.astype(F32)
    mid = r1.astype(BF)
    lo = (r1 - mid.astype(F32)).astype(BF)
    return hi, mid, lo


def _sel_dot_rhs(sel, x):
    hi, mid, lo = _split3(x)
    return _dot(sel, hi) + _dot(sel, mid) + _dot(sel, lo)


def _sel_dot_lhs(x, sel):
    hi, mid, lo = _split3(x)
    return _dot(hi, sel) + _dot(mid, sel) + _dot(lo, sel)


def _softplus(x):
    return jnp.maximum(x, 0.0) + jnp.log1p(jnp.exp(-jnp.abs(x)))


def _log_sigmoid(x):
    return jnp.minimum(x, 0.0) - jnp.log1p(jnp.exp(-jnp.abs(x)))


def _tri_masks(n):
    r = lax.broadcasted_iota(jnp.int32, (n, n), 0)
    c = lax.broadcasted_iota(jnp.int32, (n, n), 1)
    return c <= r, c >= r, c < r, c > r


def _group_ones(width, group):
    r = lax.broadcasted_iota(jnp.int32, (width, width), 0) // group
    c = lax.broadcasted_iota(jnp.int32, (width, width), 1) // group
    return jnp.where(r == c, 1.0, 0.0).astype(BF)


_IN_SEGS = ((0, 512), (512, 1024), (1024, 1536), (1536, 2048), (2048, 2816), (2816, 3072))
_IN_MAIN = 3072


def _inproj_kernel(x_ref, g_ref, w_ref, wg_ref, b_ref, o_uv, o_qk, o_v, o_o, o_c, o_z, o_g):
    x = x_ref[...]
    h = (x * lax.rsqrt(jnp.mean(x * x, axis=-1, keepdims=True) + EPS) * g_ref[...]).astype(BF)
    for o, (lo, hi) in zip((o_uv, o_qk, o_v, o_o, o_c, o_z), _IN_SEGS):
        o[...] = _dot(h, w_ref[:, lo:hi]).astype(o.dtype)
    o_g[...] = _dot(h, wg_ref[...]) + b_ref[...]


def _inproj(x2, gain, w_main, w_gate, gate_bias, tm):
    n = x2.shape[0]
    widths = [hi - lo for lo, hi in _IN_SEGS]
    out_shape = [jax.ShapeDtypeStruct((n, w), BF) for w in widths]
    out_shape.append(jax.ShapeDtypeStruct((n, LANES), F32))
    row = lambda i: (i, 0)
    const = lambda i: (0, 0)
    return pl.pallas_call(
        _inproj_kernel,
        out_shape=out_shape,
        grid=(n // tm,),
        in_specs=[
            pl.BlockSpec((tm, D_MODEL), row),
            pl.BlockSpec((1, D_MODEL), const),
            pl.BlockSpec((D_MODEL, _IN_MAIN), const),
            pl.BlockSpec((D_MODEL, LANES), const),
            pl.BlockSpec((1, LANES), const),
        ],
        out_specs=[pl.BlockSpec((tm, w), row) for w in widths] + [pl.BlockSpec((tm, LANES), row)],
        compiler_params=pltpu.CompilerParams(
            dimension_semantics=("parallel",), vmem_limit_bytes=VMEM_LIMIT),
    )(x2, gain, w_main, w_gate, gate_bias)


def _gmlp_kernel(uv_ref, gn_ref, ws_ref, bs_ref, y_ref):
    t = uv_ref.shape[0]
    gsum = _group_ones(GM_WIDTH, GM_DIM)
    rr = lax.broadcasted_iota(jnp.int32, (GM_GROUPS * GM_CHUNK, GM_WIDTH), 0) // GM_CHUNK
    cc = lax.broadcasted_iota(jnp.int32, (GM_GROUPS * GM_CHUNK, GM_WIDTH), 1) // GM_DIM
    blockdiag = rr == cc
    ws = ws_ref[...]
    bs = bs_ref[...]
    gn = gn_ref[...]

    def body(c, carry):
        off = pl.multiple_of(c * GM_CHUNK, GM_CHUNK)
        uv = uv_ref[pl.ds(off, GM_CHUNK), :].astype(F32)
        u = jax.nn.gelu(uv[:, :GM_WIDTH])
        v = jax.nn.gelu(uv[:, GM_WIDTH:])
        ms = _sel_dot_lhs(v * v, gsum) * (1.0 / GM_DIM)
        vn = (v * lax.rsqrt(ms + EPS) * gn).astype(BF)
        vexp = jnp.where(blockdiag, jnp.concatenate([vn] * GM_GROUPS, axis=0), jnp.zeros((), BF))
        sg = _dot(ws, vexp) + bs
        y_ref[pl.ds(off, GM_CHUNK), :] = (u * sg).astype(y_ref.dtype)
        return carry

    lax.fori_loop(0, t // GM_CHUNK, body, 0)


def _gmlp(uv, gn, ws_cat, bs_exp):
    b, t, _ = uv.shape
    const = lambda i: (0, 0)
    return pl.pallas_call(
        _gmlp_kernel,
        out_shape=jax.ShapeDtypeStruct((b, t, GM_WIDTH), BF),
        grid=(b,),
        in_specs=[
            pl.BlockSpec((None, t, 2 * GM_WIDTH), lambda i: (i, 0, 0)),
            pl.BlockSpec((1, GM_WIDTH), const),
            pl.BlockSpec((GM_CHUNK, GM_GROUPS * GM_CHUNK), const),
            pl.BlockSpec((GM_CHUNK, GM_WIDTH), const),
        ],
        out_specs=pl.BlockSpec((None, t, GM_WIDTH), lambda i: (i, 0, 0)),
        compiler_params=pltpu.CompilerParams(
            dimension_semantics=("parallel",), vmem_limit_bytes=VMEM_LIMIT),
    )(uv, gn, ws_cat, bs_exp)


def _mlstm_kernel(qk_ref, v_ref, og_ref, gc_ref, gr_ref, hn_ref, y_ref, acc_ref):
    t = qk_ref.shape[0]
    L = ML_CHUNK
    nc = t // L
    lower, upper, _, _ = _tri_masks(L)
    lower_b = jnp.where(lower, 1.0, 0.0).astype(BF)
    upper_b = jnp.where(upper, 1.0, 0.0).astype(BF)
    ones_v = jnp.ones((L, ML_V), BF)
    acc_ref[...] = jnp.zeros_like(acc_ref)

    def chunk_dir(cidx, d, state):
        off = pl.multiple_of(cidx * L, L)
        qkc = qk_ref[pl.ds(off, L), :]
        vc = v_ref[pl.ds(off, L), :]
        gcol = gc_ref[pl.ds(off, L), :]
        grow = gr_ref[cidx]
        mask = lower if d == 0 else upper
        bc_col = _sel_dot_rhs(lower_b if d == 0 else upper_b, _log_sigmoid(gcol))
        bc_row = _sel_dot_lhs(_log_sigmoid(grow), upper_b if d == 0 else lower_b)
        new_state = []
        for h in range(ML_HEADS):
            c_aug, m = state[h]
            q = qkc[:, h * ML_QK:(h + 1) * ML_QK]
            k = qkc[:, ML_QK_W + h * ML_QK:ML_QK_W + (h + 1) * ML_QK]
            v_aug = jnp.concatenate([vc[:, h * ML_V:(h + 1) * ML_V], ones_v], axis=1)
            qs = (q.astype(F32) * (ML_QK ** -0.5)).astype(BF)
            qk = _dot_nt(qs, k)
            ci = GATE_I + ML_HEADS * d + h
            cf = GATE_F + ML_HEADS * d + h
            ig_r = grow[ci:ci + 1, :]
            ig_c = gcol[:, ci:ci + 1]
            bc_r = bc_row[cf:cf + 1, :]
            bc_c = bc_col[:, cf:cf + 1]
            b_last = bc_r[:, L - 1:L] if d == 0 else bc_r[:, 0:1]
            a_c = b_last - bc_c + ig_c
            m_loc = jnp.max(a_c, axis=0, keepdims=True)
            kw = (k.astype(F32) * jnp.exp(a_c - m_loc)).astype(BF)
            c_loc = _dot_tn(kw, v_aug)
            m_new = jnp.maximum(b_last + m, m_loc)
            c_new = jnp.exp(b_last + m - m_new) * c_aug + jnp.exp(m_loc - m_new) * c_loc
            new_state.append((c_new, m_new))
            dmat = jnp.where(mask, bc_c - bc_r + ig_r, -jnp.inf)
            m_inter = bc_c + m
            m_t = jnp.maximum(m_inter, jnp.max(dmat, axis=1, keepdims=True))
            s = qk * jnp.exp(dmat - m_t)
            na = jnp.exp(m_inter - m_t) * _dot(qs, c_aug.astype(BF)) + _dot(s.astype(BF), v_aug)
            out = na[:, :ML_V] / jnp.maximum(jnp.abs(na[:, ML_V:]), jnp.exp(-m_t))
            acc_ref[pl.ds(off, L), h * ML_V:(h + 1) * ML_V] += out
        return tuple(new_state)

    def body(i, carry):
        fw, bw = carry
        return chunk_dir(i, 0, fw), chunk_dir(nc - 1 - i, 1, bw)

    init = tuple((jnp.zeros((ML_QK, 2 * ML_V), F32), jnp.zeros((1, 1), F32)) for _ in range(ML_HEADS))
    lax.fori_loop(0, nc, body, (init, init))

    hn = hn_ref[...]
    rows = 256

    def epilogue(j, carry):
        off = pl.multiple_of(j * rows, rows)
        hb = acc_ref[pl.ds(off, rows), :]
        gate = jax.nn.sigmoid(og_ref[pl.ds(off, rows), :].astype(F32))
        for h in range(ML_HEADS):
            sl = slice(h * ML_V, (h + 1) * ML_V)
            x = hb[:, sl]
            y = x * lax.rsqrt(jnp.mean(x * x, axis=-1, keepdims=True) + EPS) * hn[:, sl]
            y_ref[pl.ds(off, rows), sl] = (y * gate[:, sl]).astype(y_ref.dtype)
        return carry

    lax.fori_loop(0, t // rows, epilogue, 0)


def _mlstm(qk, v, og, g_col, g_row, hn):
    b, t, _ = qk.shape
    nc = t // ML_CHUNK
    seq = lambda i: (i, 0, 0)
    return pl.pallas_call(
        _mlstm_kernel,
        out_shape=jax.ShapeDtypeStruct((b, t, ML_V_W), BF),
        grid=(b,),
        in_specs=[
            pl.BlockSpec((None, t, 2 * ML_QK_W), seq),
            pl.BlockSpec((None, t, ML_V_W), seq),
            pl.BlockSpec((None, t, ML_V_W), seq),
            pl.BlockSpec((None, t, LANES), seq),
            pl.BlockSpec((None, nc, N_GATES, ML_CHUNK), lambda i: (i, 0, 0, 0)),
            pl.BlockSpec((1, ML_V_W), lambda i: (0, 0)),
        ],
        out_specs=pl.BlockSpec((None, t, ML_V_W), seq),
        scratch_shapes=[pltpu.VMEM((t, ML_V_W), F32)],
        compiler_params=pltpu.CompilerParams(
            dimension_semantics=("parallel",), vmem_limit_bytes=VMEM_LIMIT),
    )(qk, v, og, g_col, g_row, hn)


_GD_QKV_W = 2 * GD_K_W + GD_V_W
_GD_HALO = 8
_GD_PAIRS = GD_HEADS // 2
_GD_PD = 2 * _GD_PAIRS
_GD_ROWS = 16
_GD_ROW_A = 8
_GD_A_CHUNKS = 2


def _expand_pair(xb, lane_lo):
    z = jnp.zeros((), xb.dtype)
    return jnp.concatenate([jnp.where(lane_lo, xb, z), jnp.where(lane_lo, z, xb)], axis=0)


def _gdn_kernel(c_ref, z_ref, gc_ref, gr_ref, cw_ref, acol_ref, arow_ref, hn_ref, y_ref,
                xp_ref, qkv_ref, acc_ref, u_ref, wq_ref, att_ref, kd_ref, egl_ref):
    t = c_ref.shape[0]
    L = GD_CHUNK
    nc = t // L
    incl_lo, incl_up, _, _ = _tri_masks(L)
    lower_b = jnp.where(incl_lo, 1.0, 0.0).astype(BF)
    upper_b = jnp.where(incl_up, 1.0, 0.0).astype(BF)
    gsum = _group_ones(LANES, GD_K)

    xp_ref[0:_GD_HALO, :] = jnp.zeros((_GD_HALO, _GD_QKV_W), F32)
    xp_ref[t + _GD_HALO:t + 2 * _GD_HALO, :] = jnp.zeros((_GD_HALO, _GD_QKV_W), F32)
    xp_ref[_GD_HALO:t + _GD_HALO, :] = c_ref[...].astype(F32)
    cw = cw_ref[...]
    rows = 256
    pad = (GD_CONV - 1) // 2

    def conv_tile(j, carry):
        off = pl.multiple_of(j * rows, rows)
        xt = xp_ref[pl.ds(off, rows + 2 * _GD_HALO), :]
        y = jnp.zeros((rows, _GD_QKV_W), F32)
        for tap in range(GD_CONV):
            s0 = _GD_HALO + tap - pad
            y = y + cw[tap:tap + 1, :] * xt[s0:s0 + rows, :]
        y = y * jax.nn.sigmoid(y)
        for s in range(2 * GD_K_W // LANES):
            sl = slice(s * LANES, (s + 1) * LANES)
            ys = y[:, sl]
            inv = lax.rsqrt(_sel_dot_lhs(ys * ys, gsum) + EPS)
            scale = GD_K ** -0.5 if s < GD_K_W // LANES else 1.0
            qkv_ref[pl.ds(off, rows), sl] = ys * inv * scale
        qkv_ref[pl.ds(off, rows), 2 * GD_K_W:] = y[:, 2 * GD_K_W:]
        return carry

    lax.fori_loop(0, t // rows, conv_tile, 0)

    tt = lax.broadcasted_iota(jnp.int32, (L, LANES), 0)
    ln = lax.broadcasted_iota(jnp.int32, (L, LANES), 1)
    ss = ln % GD_K
    lane_lo = ln < GD_K
    incl = (ss <= tt, ss >= tt)
    strict = (ss < tt, ss > tt)
    bw_cols = (ln // GD_HEADS) % 2 == 1
    r2 = lax.broadcasted_iota(jnp.int32, (2 * L, LANES), 0)
    c2 = lax.broadcasted_iota(jnp.int32, (2 * L, LANES), 1)
    same_half = (r2 // L) == (c2 // GD_K)
    bd_upper = jnp.where(same_half & (c2 % GD_K >= r2 % L), 1.0, 0.0).astype(BF)
    bd_lower = jnp.where(same_half & (c2 % GD_K <= r2 % L), 1.0, 0.0).astype(BF)
    ej = lax.broadcasted_iota(jnp.int32, (LANES, _GD_PD * LANES), 0)
    en = lax.broadcasted_iota(jnp.int32, (LANES, _GD_PD * LANES), 1) // GD_K
    e_a = jnp.where(ej == GATE_A + en, 1.0, 0.0).astype(BF)
    e_b = jnp.where(ej == GATE_B + en, 1.0, 0.0).astype(BF)
    a_col = jnp.exp(acol_ref[...])
    a_row = jnp.exp(arow_ref[...])

    def phase_a(j, carry):
        chains = []
        for ci in range(_GD_A_CHUNKS):
            cidx = j * _GD_A_CHUNKS + ci
            off = pl.multiple_of(cidx * L, L)
            gcol = gc_ref[pl.ds(off, L), :]
            grow = gr_ref[cidx]
            qkv = qkv_ref[pl.ds(off, L), :]
            g_c = -a_col * _softplus(gcol)
            cum_c = jnp.where(bw_cols, _sel_dot_rhs(upper_b, g_c), _sel_dot_rhs(lower_b, g_c))
            gcb_all = _sel_dot_lhs(cum_c, e_a)
            beb_all = _sel_dot_lhs(jax.nn.sigmoid(gcol), e_b)
            g_r = -a_row * _softplus(grow)
            cum_r = (_sel_dot_lhs(g_r, bd_upper), _sel_dot_lhs(g_r, bd_lower))
            for p in range(_GD_PAIRS):
                q_p = qkv[:, p * LANES:(p + 1) * LANES]
                k_p = qkv[:, GD_K_W + p * LANES:GD_K_W + (p + 1) * LANES]
                v_p = qkv[:, 2 * GD_K_W + p * LANES:2 * GD_K_W + (p + 1) * LANES]
                kexp = _expand_pair(k_p.astype(BF), lane_lo)
                kk = _dot_nt(k_p.astype(BF), kexp)
                qk = _dot_nt(q_p.astype(BF), kexp)
                for d in range(2):
                    pd = d * _GD_PAIRS + p
                    gcb = gcb_all[:, pd * LANES:(pd + 1) * LANES]
                    beb = beb_all[:, pd * LANES:(pd + 1) * LANES]
                    gcr = cum_r[d][_GD_ROW_A + pd:_GD_ROW_A + pd + 1, :]
                    decay = jnp.exp(jnp.where(incl[d], gcb - gcr, -jnp.inf))
                    lmat = jnp.where(strict[d], beb * kk * decay, 0.0)
                    eg = jnp.exp(gcb)
                    gl = gcb[L - 1:L, :] if d == 0 else gcb[0:1, :]
                    x = jnp.concatenate([v_p * beb, k_p * (beb * eg)], axis=1)
                    att_ref[pd, pl.ds(off, L), :] = (qk * decay).astype(BF)
                    wq_ref[pd, cidx, L:2 * L, :] = (q_p * eg).astype(BF)
                    kd_ref[pd, pl.ds(off, L), :] = (k_p * jnp.exp(gl - gcb)).astype(BF)
                    egl_ref[cidx, pd] = jnp.broadcast_to(jnp.exp(gl), (8, LANES))
                    chains.append(dict(pd=pd, cidx=cidx, off=off, p=lmat.astype(BF), x=x))

        def apply(ch, sign):
            xb = ch["x"].astype(BF)
            rhs = jnp.concatenate([_expand_pair(xb[:, :LANES], lane_lo),
                                   _expand_pair(xb[:, LANES:], lane_lo)], axis=1)
            return ch["x"] + sign * _dot(ch["p"], rhs)

        for ch in chains:
            ch["x"] = apply(ch, -1.0)
        power = 2
        while power < L:
            for ch in chains:
                ch["p"] = _dot(ch["p"], _expand_pair(ch["p"], lane_lo)).astype(BF)
            for ch in chains:
                ch["x"] = apply(ch, 1.0)
            power *= 2
        for ch in chains:
            u_ref[ch["pd"], pl.ds(ch["off"], L), :] = ch["x"][:, :LANES]
            wq_ref[ch["pd"], ch["cidx"], 0:L, :] = ch["x"][:, LANES:].astype(BF)
        return carry

    lax.fori_loop(0, nc // _GD_A_CHUNKS, phase_a, 0)

    acc_ref[...] = jnp.zeros_like(acc_ref)
    rb = lax.broadcasted_iota(jnp.int32, (LANES, LANES), 0) // GD_K
    cb = lax.broadcasted_iota(jnp.int32, (LANES, LANES), 1) // GD_V
    blockdiag = rb == cb

    def phase_b(i, states):
        items = []
        for d in range(2):
            cidx = i if d == 0 else nc - 1 - i
            off = pl.multiple_of(cidx * L, L)
            for p in range(_GD_PAIRS):
                pd = d * _GD_PAIRS + p
                items.append((pd, p, cidx, off))
        s_b = [s.astype(BF) for s in states]
        res = [_dot(wq_ref[pd, cidx], s_b[pd]) for pd, _, cidx, off in items]
        v_new = [(u_ref[pd, pl.ds(off, L), :] - res[pd][:L]).astype(BF) for pd, _, _, off in items]
        new_states = []
        for pd, p, cidx, off in items:
            upd = _dot_tn(kd_ref[pd, pl.ds(off, L), :], v_new[pd])
            new_states.append(states[pd] * egl_ref[cidx, pd][0:1, :] + jnp.where(blockdiag, upd, 0.0))
        for pd, p, cidx, off in items:
            o = res[pd][L:] + _dot(att_ref[pd, pl.ds(off, L), :], _expand_pair(v_new[pd], lane_lo))
            acc_ref[pl.ds(off, L), p * LANES:(p + 1) * LANES] += o
        return tuple(new_states)

    init = tuple(jnp.zeros((LANES, LANES), F32) for _ in range(_GD_PD))
    lax.fori_loop(0, nc, phase_b, init)

    hn = hn_ref[...]
    gsum_v = _group_ones(GD_V_W, GD_V)

    def epilogue(j, carry):
        off = pl.multiple_of(j * rows, rows)
        x = acc_ref[pl.ds(off, rows), :]
        z = z_ref[pl.ds(off, rows), :].astype(F32)
        ms = _sel_dot_lhs(x * x, gsum_v) * (1.0 / GD_V)
        y = x * lax.rsqrt(ms + EPS) * hn
        y_ref[pl.ds(off, rows), :] = (y * (z * jax.nn.sigmoid(z))).astype(y_ref.dtype)
        return carry

    lax.fori_loop(0, t // rows, epilogue, 0)


def _gdn(c_qkv, z, g_col, g_rowp, conv_w, a_col, a_rowp, hn):
    b, t, _ = c_qkv.shape
    nc = t // GD_CHUNK
    seq = lambda i: (i, 0, 0)
    const = lambda i: (0, 0)
    return pl.pallas_call(
        _gdn_kernel,
        out_shape=jax.ShapeDtypeStruct((b, t, GD_V_W), BF),
        grid=(b,),
        in_specs=[
            pl.BlockSpec((None, t, _GD_QKV_W), seq),
            pl.BlockSpec((None, t, GD_V_W), seq),
            pl.BlockSpec((None, t, LANES), seq),
            pl.BlockSpec((None, nc, _GD_ROWS, LANES), lambda i: (i, 0, 0, 0)),
            pl.BlockSpec((GD_CONV, _GD_QKV_W), const),
            pl.BlockSpec((1, LANES), const),
            pl.BlockSpec((_GD_ROWS, LANES), const),
            pl.BlockSpec((1, GD_V_W), const),
        ],
        out_specs=pl.BlockSpec((None, t, GD_V_W), seq),
        scratch_shapes=[
            pltpu.VMEM((t + 2 * _GD_HALO, _GD_QKV_W), F32),
            pltpu.VMEM((t, _GD_QKV_W), F32),
            pltpu.VMEM((t, GD_V_W), F32),
            pltpu.VMEM((_GD_PD, t, LANES), F32),
            pltpu.VMEM((_GD_PD, nc, 2 * GD_CHUNK, LANES), BF),
            pltpu.VMEM((_GD_PD, t, LANES), BF),
            pltpu.VMEM((_GD_PD, t, LANES), BF),
            pltpu.VMEM((nc, _GD_PD, 8, LANES), F32),
        ],
        compiler_params=pltpu.CompilerParams(
            dimension_semantics=("parallel",), vmem_limit_bytes=VMEM_LIMIT),
    )(c_qkv, z, g_col, g_rowp, conv_w, a_col, a_rowp, hn)


def _outproj_kernel(x_ref, ya_ref, yb_ref, yc_ref, w_ref, o_ref):
    acc = x_ref[...]
    acc = acc + _dot(ya_ref[...], w_ref[0:GM_WIDTH, :])
    acc = acc + _dot(yb_ref[...], w_ref[GM_WIDTH:GM_WIDTH + ML_V_W, :])
    acc = acc + _dot(yc_ref[...], w_ref[GM_WIDTH + ML_V_W:, :])
    o_ref[...] = acc


def _outproj(x2, ya, yb, yc, w, tm):
    n = x2.shape[0]
    row = lambda i: (i, 0)
    return pl.pallas_call(
        _outproj_kernel,
        out_shape=jax.ShapeDtypeStruct((n, D_MODEL), F32),
        grid=(n // tm,),
        in_specs=[
            pl.BlockSpec((tm, D_MODEL), row),
            pl.BlockSpec((tm, GM_WIDTH), row),
            pl.BlockSpec((tm, ML_V_W), row),
            pl.BlockSpec((tm, GD_V_W), row),
            pl.BlockSpec((D_MODEL, D_MODEL), lambda i: (0, 0)),
        ],
        out_specs=pl.BlockSpec((tm, D_MODEL), row),
        compiler_params=pltpu.CompilerParams(
            dimension_semantics=("parallel",), vmem_limit_bytes=VMEM_LIMIT),
    )(x2, ya, yb, yc, w)


_FFN_HALO = 16
_FFN_CHUNK = 256


def _ffn_kernel(xm_ref, xp_ref, xn_ref, g_ref, wup_ref, cw_ref, cb_ref, wdn_ref, gf_ref, o_ref,
                h_ref, acc_ref, *, tiles_per_seq, final_norm):
    i = pl.program_id(0)
    tm = xm_ref.shape[0]
    g = g_ref[...]
    pos = i % tiles_per_seq

    def norm(x):
        return x * lax.rsqrt(jnp.mean(x * x, axis=-1, keepdims=True) + EPS) * g

    hp = jnp.where(pos == 0, 0.0, norm(xp_ref[...]))
    hn = jnp.where(pos == tiles_per_seq - 1, 0.0, norm(xn_ref[...]))
    xm = xm_ref[...]
    h_ref[0:_FFN_HALO, :] = hp.astype(BF)
    h_ref[_FFN_HALO:_FFN_HALO + tm, :] = norm(xm).astype(BF)
    h_ref[_FFN_HALO + tm:, :] = hn.astype(BF)
    hh = h_ref[...]
    cw = cw_ref[...]
    cb = cb_ref[...]
    acc_ref[...] = xm

    def conv(u, c0):
        sl = slice(c0, c0 + _FFN_CHUNK)
        return (cw[0:1, sl] * u[_FFN_HALO - 1:_FFN_HALO - 1 + tm, :]
                + cw[1:2, sl] * u[_FFN_HALO:_FFN_HALO + tm, :]
                + cw[2:3, sl] * u[_FFN_HALO + 1:_FFN_HALO + 1 + tm, :] + cb[:, sl])

    for j in range(D_FF // _FFN_CHUNK):
        c0 = j * _FFN_CHUNK
        gt = conv(_dot(hh, wup_ref[:, c0:c0 + _FFN_CHUNK]), c0)
        vl = conv(_dot(hh, wup_ref[:, D_FF + c0:D_FF + c0 + _FFN_CHUNK]), D_FF + c0)
        act = (gt * jax.nn.sigmoid(gt) * vl).astype(BF)
        acc_ref[...] += _dot(act, wdn_ref[c0:c0 + _FFN_CHUNK, :])

    y = acc_ref[...]
    if final_norm:
        y = y * lax.rsqrt(jnp.mean(y * y, axis=-1, keepdims=True) + EPS) * gf_ref[...]
    o_ref[...] = y


def _ffn(x2, gain, w_up, conv_w, conv_b, w_down, gain_final, seq_len, tm, final_norm):
    n = x2.shape[0]
    hb = tm // _FFN_HALO
    nhb = n // _FFN_HALO
    row = lambda i: (i, 0)
    const = lambda i: (0, 0)
    kern = functools.partial(_ffn_kernel, tiles_per_seq=seq_len // tm, final_norm=final_norm)
    return pl.pallas_call(
        kern,
        out_shape=jax.ShapeDtypeStruct((n, D_MODEL), F32),
        grid=(n // tm,),
        in_specs=[
            pl.BlockSpec((tm, D_MODEL), row),
            pl.BlockSpec((_FFN_HALO, D_MODEL), lambda i: (jnp.maximum(i * hb - 1, 0), 0)),
            pl.BlockSpec((_FFN_HALO, D_MODEL), lambda i: (jnp.minimum((i + 1) * hb, nhb - 1), 0)),
            pl.BlockSpec((1, D_MODEL), const),
            pl.BlockSpec((D_MODEL, 2 * D_FF), const, pipeline_mode=pl.Buffered(1)),
            pl.BlockSpec((FFN_CONV, 2 * D_FF), const),
            pl.BlockSpec((1, 2 * D_FF), const),
            pl.BlockSpec((D_FF, D_MODEL), const, pipeline_mode=pl.Buffered(1)),
            pl.BlockSpec((1, D_MODEL), const),
        ],
        out_specs=pl.BlockSpec((tm, D_MODEL), row),
        scratch_shapes=[
            pltpu.VMEM((tm + 2 * _FFN_HALO, D_MODEL), BF),
            pltpu.VMEM((tm, D_MODEL), F32),
        ],
        compiler_params=pltpu.CompilerParams(
            dimension_semantics=("parallel",), vmem_limit_bytes=VMEM_LIMIT),
    )(x2, x2, x2, gain, w_up, conv_w, conv_b, w_down, gain_final)


_TM = 512


def _layer(x2, batch, seq_len, norm_mix, w_in, gm_norm, gm_ws, gm_bs, ml_gate_bias, ml_head_norm,
           gd_conv, gd_a_log, gd_dt_bias, gd_head_norm, w_out, norm_ffn, w_up, ffn_conv, ffn_conv_b,
           w_down, norm_final, final_norm):
    n = x2.shape[0]
    nc = seq_len // GD_CHUNK
    g0 = 2 * GM_WIDTH + 2 * ML_QK_W + 2 * ML_V_W
    g1 = g0 + 4 * ML_HEADS
    c1 = g1 + _GD_QKV_W + GD_V_W
    w_main = jnp.concatenate([w_in[:, :g0], w_in[:, g1:c1]], axis=1).astype(BF)
    w_gate = jnp.concatenate(
        [w_in[:, g0:g1], w_in[:, c1:], jnp.zeros((D_MODEL, LANES - N_GATES), F32)], axis=1).astype(BF)
    gate_bias = jnp.concatenate(
        [ml_gate_bias, gd_dt_bias.reshape(-1), jnp.zeros((LANES - GATE_B,), F32)]).reshape(1, LANES)
    uv, qk, v, og, cq, z, gates = _inproj(x2, norm_mix.reshape(1, D_MODEL), w_main, w_gate, gate_bias, _TM)

    seq = lambda a: a.reshape(batch, seq_len, a.shape[-1])
    g_col = seq(gates)
    g32 = g_col[:, :, :N_GATES]
    g_row = g32.reshape(batch, nc, GD_CHUNK, N_GATES).transpose(0, 1, 3, 2)
    g_rowp = g32.reshape(batch, nc, GD_CHUNK, 4, 2, _GD_PAIRS, 2).transpose(0, 1, 3, 4, 5, 6, 2)
    g_rowp = g_rowp.reshape(batch, nc, _GD_ROWS, LANES)

    ws_cat = gm_ws.transpose(1, 0, 2).reshape(GM_CHUNK, GM_GROUPS * GM_CHUNK).astype(BF)
    bs_exp = jnp.repeat(gm_bs.T, GM_DIM, axis=1)
    ya = _gmlp(seq(uv), gm_norm.reshape(1, GM_WIDTH), ws_cat, bs_exp)

    yb = _mlstm(seq(qk), seq(v), seq(og), g_col, g_row, ml_head_norm.reshape(1, ML_V_W))

    a_flat = gd_a_log.reshape(-1)
    a_col = jnp.zeros((LANES,), F32).at[GATE_A:GATE_B].set(a_flat).reshape(1, LANES)
    a_rowp = jnp.zeros((_GD_ROWS, LANES), F32).at[_GD_ROW_A:_GD_ROW_A + _GD_PD].set(
        jnp.repeat(a_flat.reshape(_GD_PD, 2), GD_K, axis=1))
    yc = _gdn(seq(cq), seq(z), g_col, g_rowp, gd_conv, a_col, a_rowp, gd_head_norm.reshape(1, GD_V_W))

    flat = lambda a: a.reshape(n, a.shape[-1])
    x2 = _outproj(x2, flat(ya), flat(yb), flat(yc), w_out.astype(BF), _TM)
    return _ffn(x2, norm_ffn.reshape(1, D_MODEL), w_up.astype(BF), ffn_conv, ffn_conv_b.reshape(1, -1),
                w_down.astype(BF), norm_final.reshape(1, D_MODEL), seq_len, _TM, final_norm)


def kernel(x, norm_mix, w_in, gm_norm, gm_ws, gm_bs, ml_gate_bias, ml_head_norm, gd_conv, gd_A_log,
           gd_dt_bias, gd_head_norm, w_out, norm_ffn, w_up, ffn_conv, ffn_conv_b, w_down, norm_final):
    batch, seq_len, _ = x.shape
    depth = w_in.shape[0]
    x2 = x.reshape(batch * seq_len, D_MODEL)
    for i in range(depth):
        x2 = _layer(x2, batch, seq_len, norm_mix[i], w_in[i], gm_norm[i], gm_ws[i], gm_bs[i],
                    ml_gate_bias[i], ml_head_norm[i], gd_conv[i], gd_A_log[i], gd_dt_bias[i],
                    gd_head_norm[i], w_out[i], norm_ffn[i], w_up[i], ffn_conv[i], ffn_conv_b[i],
                    w_down[i], norm_final, i == depth - 1)
    return x2.reshape(batch, seq_len, D_MODEL)
```

```python
import functools

import jax
import jax.numpy as jnp
from jax import lax
from jax.experimental import pallas as pl
from jax.experimental.pallas import tpu as pltpu

D_MODEL = 1024
GM_GROUPS = 4
GM_DIM = 64
GM_WIDTH = GM_GROUPS * GM_DIM
GM_CHUNK = 128
ML_HEADS = 4
ML_QK = 64
ML_V = 128
ML_QK_W = ML_HEADS * ML_QK
ML_V_W = ML_HEADS * ML_V
ML_CHUNK = 64
GD_HEADS = 4
GD_K = 64
GD_V = 64
GD_K_W = GD_HEADS * GD_K
GD_V_W = GD_HEADS * GD_V
GD_CHUNK = 64
GD_CONV = 5
D_FF = 2816
FFN_CONV = 3
EPS = 1e-6

LANES = 128
N_GATES = 32
GATE_I, GATE_F, GATE_A, GATE_B = 0, 8, 16, 24
PAIR_ROWS = 16
VMEM_LIMIT = 56 * 1024 * 1024

BF = jnp.bfloat16
F32 = jnp.float32


def _dot(a, b):
    return jnp.dot(a, b, preferred_element_type=F32)


def _dot_nt(a, b):
    return lax.dot_general(a, b, (((1,), (1,)), ((), ())), preferred_element_type=F32)


def _dot_tn(a, b):
    return lax.dot_general(a, b, (((0,), (0,)), ((), ())), preferred_element_type=F32)


def _split(x, parts):
    out = []
    for _ in range(parts - 1):
        hi = x.astype(BF)
        out.append(hi)
        x = x - hi.astype(F32)
    out.append(x.astype(BF))
    return out


def _sel_dot_rhs(sel, x, parts=3):
    acc = None
    for term in _split(x, parts):
        y = _dot(sel, term)
        acc = y if acc is None else acc + y
    return acc


def _sel_dot_lhs(x, sel, parts=3):
    acc = None
    for term in _split(x, parts):
        y = _dot(term, sel)
        acc = y if acc is None else acc + y
    return acc


def _expand_pair(xb, lane_lo):
    z = jnp.zeros((), xb.dtype)
    return jnp.concatenate([jnp.where(lane_lo, xb, z), jnp.where(lane_lo, z, xb)], axis=0)


def _run_interleaved(gens):
    results = [None] * len(gens)
    pending = list(range(len(gens)))
    while pending:
        for k in list(pending):
            try:
                next(gens[k])
            except StopIteration as stop:
                results[k] = stop.value
                pending.remove(k)
    return tuple(results)


def _running_max(x, reverse):
    n, w = x.shape
    sh = 1
    while sh < n:
        fill = jnp.full((sh, w), -jnp.inf, x.dtype)
        if reverse:
            shifted = jnp.concatenate([x[sh:], fill], axis=0)
        else:
            shifted = jnp.concatenate([fill, x[:n - sh]], axis=0)
        x = jnp.maximum(x, shifted)
        sh *= 2
    return x


def _pair_cumsum_mats():
    r = lax.broadcasted_iota(jnp.int32, (LANES, LANES), 0)
    c = lax.broadcasted_iota(jnp.int32, (LANES, LANES), 1)
    half = LANES // 2
    same = (r // half) == (c // half)
    prefix = jnp.where(same & (c >= r), 1.0, 0.0).astype(BF)
    suffix = jnp.where(same & (c <= r), 1.0, 0.0).astype(BF)
    return prefix, suffix


def _column_spread(first_col, n_cols, width):
    j = lax.broadcasted_iota(jnp.int32, (LANES, n_cols * width), 0)
    n = lax.broadcasted_iota(jnp.int32, (LANES, n_cols * width), 1) // width
    return jnp.where(j == first_col + n, 1.0, 0.0).astype(BF)


def _softplus(x):
    return jnp.maximum(x, 0.0) + jnp.log1p(jnp.exp(-jnp.abs(x)))


def _log_sigmoid(x):
    return jnp.minimum(x, 0.0) - jnp.log1p(jnp.exp(-jnp.abs(x)))


def _tri_masks(n):
    r = lax.broadcasted_iota(jnp.int32, (n, n), 0)
    c = lax.broadcasted_iota(jnp.int32, (n, n), 1)
    return c <= r, c >= r, c < r, c > r


def _group_ones(width, group):
    r = lax.broadcasted_iota(jnp.int32, (width, width), 0) // group
    c = lax.broadcasted_iota(jnp.int32, (width, width), 1) // group
    return jnp.where(r == c, 1.0, 0.0).astype(BF)


_IN_SEGS = ((0, 512), (512, 1024), (1024, 1536), (1536, 2048), (2048, 2816), (2816, 3072))
_IN_MAIN = 3072


def _inproj_kernel(x_ref, g_ref, w_ref, wg_ref, b_ref, o_uv, o_qk, o_v, o_o, o_c, o_z, o_g):
    x = x_ref[...]
    h = (x * lax.rsqrt(jnp.mean(x * x, axis=-1, keepdims=True) + EPS) * g_ref[...]).astype(BF)
    for o, (lo, hi) in zip((o_uv, o_qk, o_v, o_o, o_c, o_z), _IN_SEGS):
        o[...] = _dot(h, w_ref[:, lo:hi]).astype(o.dtype)
    o_g[...] = _dot(h, wg_ref[...]) + b_ref[...]


def _inproj(x2, gain, w_main, w_gate, gate_bias, tm):
    n = x2.shape[0]
    widths = [hi - lo for lo, hi in _IN_SEGS]
    out_shape = [jax.ShapeDtypeStruct((n, w), BF) for w in widths]
    out_shape.append(jax.ShapeDtypeStruct((n, LANES), F32))
    row = lambda i: (i, 0)
    const = lambda i: (0, 0)
    return pl.pallas_call(
        _inproj_kernel,
        out_shape=out_shape,
        grid=(n // tm,),
        in_specs=[
            pl.BlockSpec((tm, D_MODEL), row),
            pl.BlockSpec((1, D_MODEL), const),
            pl.BlockSpec((D_MODEL, _IN_MAIN), const),
            pl.BlockSpec((D_MODEL, LANES), const),
            pl.BlockSpec((1, LANES), const),
        ],
        out_specs=[pl.BlockSpec((tm, w), row) for w in widths] + [pl.BlockSpec((tm, LANES), row)],
        compiler_params=pltpu.CompilerParams(
            dimension_semantics=("parallel",), vmem_limit_bytes=VMEM_LIMIT),
    )(x2, gain, w_main, w_gate, gate_bias)


def _gmlp_kernel(uv_ref, gn_ref, ws_ref, bs_ref, y_ref):
    t = uv_ref.shape[0]
    gsum = _group_ones(GM_WIDTH, GM_DIM)
    rr = lax.broadcasted_iota(jnp.int32, (GM_GROUPS * GM_CHUNK, GM_WIDTH), 0) // GM_CHUNK
    cc = lax.broadcasted_iota(jnp.int32, (GM_GROUPS * GM_CHUNK, GM_WIDTH), 1) // GM_DIM
    blockdiag = rr == cc
    ws = ws_ref[...]
    bs = bs_ref[...]
    gn = gn_ref[...]

    def body(c, carry):
        off = pl.multiple_of(c * GM_CHUNK, GM_CHUNK)
        uv = uv_ref[pl.ds(off, GM_CHUNK), :].astype(F32)
        u = jax.nn.gelu(uv[:, :GM_WIDTH])
        v = jax.nn.gelu(uv[:, GM_WIDTH:])
        ms = _sel_dot_lhs(v * v, gsum) * (1.0 / GM_DIM)
        vn = (v * lax.rsqrt(ms + EPS) * gn).astype(BF)
        vexp = jnp.where(blockdiag, jnp.concatenate([vn] * GM_GROUPS, axis=0), jnp.zeros((), BF))
        sg = _dot(ws, vexp) + bs
        y_ref[pl.ds(off, GM_CHUNK), :] = (u * sg).astype(y_ref.dtype)
        return carry

    lax.fori_loop(0, t // GM_CHUNK, body, 0)


def _gmlp(uv, gn, ws_cat, bs_exp):
    b, t, _ = uv.shape
    const = lambda i: (0, 0)
    return pl.pallas_call(
        _gmlp_kernel,
        out_shape=jax.ShapeDtypeStruct((b, t, GM_WIDTH), BF),
        grid=(b,),
        in_specs=[
            pl.BlockSpec((None, t, 2 * GM_WIDTH), lambda i: (i, 0, 0)),
            pl.BlockSpec((1, GM_WIDTH), const),
            pl.BlockSpec((GM_CHUNK, GM_GROUPS * GM_CHUNK), const),
            pl.BlockSpec((GM_CHUNK, GM_WIDTH), const),
        ],
        out_specs=pl.BlockSpec((None, t, GM_WIDTH), lambda i: (i, 0, 0)),
        compiler_params=pltpu.CompilerParams(
            dimension_semantics=("parallel",), vmem_limit_bytes=VMEM_LIMIT),
    )(uv, gn, ws_cat, bs_exp)


def _mlstm_kernel(qk_ref, v_ref, og_ref, gc_ref, gr_ref, hn_ref, y_ref,
                  acc_ref, bc_ref, cm_ref, wa_ref, st_ref):
    t = qk_ref.shape[0]
    L = ML_CHUNK
    nc = t // L
    lower, upper, _, _ = _tri_masks(L)
    lower_b = jnp.where(lower, 1.0, 0.0).astype(BF)
    upper_b = jnp.where(upper, 1.0, 0.0).astype(BF)
    ones_v = jnp.ones((L, ML_V), BF)
    zero_v = jnp.zeros((L, ML_V), BF)
    zero_c = jnp.zeros((ML_QK, ML_V), BF)
    tt = lax.broadcasted_iota(jnp.int32, (L, LANES), 0)
    ln = lax.broadcasted_iota(jnp.int32, (L, LANES), 1)
    lane_lo = ln < ML_QK
    mask = (ln % ML_QK <= tt, ln % ML_QK >= tt)
    f_cols = (ln >= GATE_F) & (ln < GATE_F + 2 * ML_HEADS)
    pair_prefix, pair_suffix = _pair_cumsum_mats()
    ro = lax.broadcasted_iota(jnp.int32, (LANES, 2 * ML_V), 0) // ML_QK
    co = lax.broadcasted_iota(jnp.int32, (LANES, 2 * ML_V), 1) // ML_V
    pair_rowsum = jnp.where(ro == co, 1.0, 0.0).astype(BF)
    bw_cols = (ln // ML_HEADS) % 2 == 1

    def gate_stats(c, carry):
        off = pl.multiple_of(c * L, L)
        gcol = gc_ref[pl.ds(off, L), :]
        lf = _log_sigmoid(gcol)
        bc = jnp.where(bw_cols, _sel_dot_rhs(upper_b, lf), _sel_dot_rhs(lower_b, lf))
        ig = pltpu.roll(gcol, GATE_F - GATE_I, 1)
        b_last = jnp.where(bw_cols[0:1], bc[0:1, :], bc[L - 1:L, :])
        a = b_last - bc + ig
        m_loc = jnp.max(a, axis=0, keepdims=True)
        r = ig - bc
        bc_ref[pl.ds(off, L), :] = bc
        cm_ref[pl.ds(off, L), :] = jnp.where(bw_cols, _running_max(r, True), _running_max(r, False))
        wa_ref[pl.ds(off, L), :] = jnp.exp(a - m_loc)
        st_ref[c] = jnp.concatenate([b_last, m_loc, jnp.zeros((6, LANES), F32)], axis=0)
        return carry

    lax.fori_loop(0, nc, gate_stats, 0, unroll=4)

    def chunk_dir(cidx, d, state):
        heads, m = state
        off = pl.multiple_of(cidx * L, L)
        qkc = qk_ref[pl.ds(off, L), :]
        vc = v_ref[pl.ds(off, L), :]
        grow = gr_ref[cidx]
        col0 = GATE_F + ML_HEADS * d
        spread_k = _column_spread(col0, ML_HEADS, ML_QK)
        spread_v = _column_spread(col0, ML_HEADS, ML_V)
        bc = bc_ref[pl.ds(off, L), :]
        wa = wa_ref[pl.ds(off, L), :]
        b_last = st_ref[cidx][0:1, :]
        m_loc = st_ref[cidx][1:2, :]
        mx = jnp.maximum(m, cm_ref[pl.ds(off, L), :])
        m_new = jnp.maximum(b_last + m, m_loc)
        f_rows = jnp.concatenate(
            [jnp.exp(b_last + m - m_new), jnp.exp(m_loc - m_new), jnp.zeros((6, LANES), F32)], axis=0)
        clean = lambda x: jnp.where(f_cols[:x.shape[0]], x, 0.0)
        mx_k = _sel_dot_lhs(clean(mx), spread_k, parts=2)
        wa_k = _sel_dot_lhs(clean(wa), spread_k, parts=1)
        w_inter = jnp.exp(_sel_dot_lhs(clean(m - mx), spread_v, parts=2))
        e_mt = jnp.exp(_sel_dot_lhs(clean(-(bc + mx)), spread_v, parts=2))
        f_v = _sel_dot_lhs(clean(f_rows), spread_v)
        cum_r = _sel_dot_lhs(_log_sigmoid(grow), pair_prefix if d == 0 else pair_suffix)
        yield
        new_heads = []
        outs = []
        for p in range(ML_HEADS // 2):
            row = 2 * d + p
            q_p = qkc[:, p * LANES:(p + 1) * LANES]
            k_p = qkc[:, ML_QK_W + p * LANES:ML_QK_W + (p + 1) * LANES]
            v0 = vc[:, 2 * p * ML_V:(2 * p + 1) * ML_V]
            v1 = vc[:, (2 * p + 1) * ML_V:(2 * p + 2) * ML_V]
            (c0, n0), (c1, n1) = heads[2 * p], heads[2 * p + 1]
            qk = _dot_nt(q_p, _expand_pair(k_p, lane_lo))
            rr = grow[row:row + 1, :] - cum_r[ML_HEADS + row:ML_HEADS + row + 1, :]
            s = qk * jnp.exp(jnp.where(mask[d], rr - mx_k[:, p * LANES:(p + 1) * LANES], -jnp.inf))
            rhs_s = jnp.concatenate(
                [jnp.concatenate([v0, zero_v], axis=1), jnp.concatenate([zero_v, v1], axis=1)], axis=0)
            intra = _dot(s.astype(BF), jnp.concatenate([rhs_s, pair_rowsum], axis=1))
            rhs_c = jnp.concatenate(
                [jnp.concatenate([c0.astype(BF), zero_c, n0.astype(BF), zero_c], axis=1),
                 jnp.concatenate([zero_c, c1.astype(BF), zero_c, n1.astype(BF)], axis=1)], axis=0)
            inter = _dot(q_p, rhs_c)
            w2 = w_inter[:, 2 * p * ML_V:(2 * p + 2) * ML_V]
            na = jnp.concatenate([w2, w2], axis=1) * inter + intra
            out = na[:, :2 * ML_V] / jnp.maximum(jnp.abs(na[:, 2 * ML_V:]), e_mt[:, 2 * p * ML_V:(2 * p + 2) * ML_V])
            outs.append(out)
            yield
            kw = (k_p.astype(F32) * wa_k[:, p * LANES:(p + 1) * LANES]).astype(BF)
            upd = _dot_tn(kw, jnp.concatenate([v0, v1, ones_v], axis=1))
            for hh, (c, n) in enumerate(((c0, n0), (c1, n1))):
                h = 2 * p + hh
                f_old = f_v[0:1, h * ML_V:(h + 1) * ML_V]
                f_loc = f_v[1:2, h * ML_V:(h + 1) * ML_V]
                rs = slice(hh * ML_QK, (hh + 1) * ML_QK)
                new_heads.append((f_old * c + f_loc * upd[rs, hh * ML_V:(hh + 1) * ML_V],
                                  f_old * n + f_loc * upd[rs, 2 * ML_V:3 * ML_V]))
            yield
        for p, out in enumerate(outs):
            acc_ref[d, pl.ds(off, L), 2 * p * ML_V:(2 * p + 2) * ML_V] = out
        return tuple(new_heads), m_new

    def body(i, carry):
        return _run_interleaved([chunk_dir(i, 0, carry[0]), chunk_dir(nc - 1 - i, 1, carry[1])])

    zero_state = jnp.zeros((ML_QK, ML_V), F32)
    init = (tuple((zero_state, zero_state) for _ in range(ML_HEADS)), jnp.zeros((1, LANES), F32))
    lax.fori_loop(0, nc, body, (init, init), unroll=4)

    hn = hn_ref[...]
    rows = 256

    def epilogue(j, carry):
        off = pl.multiple_of(j * rows, rows)
        hb = acc_ref[0, pl.ds(off, rows), :] + acc_ref[1, pl.ds(off, rows), :]
        gate = jax.nn.sigmoid(og_ref[pl.ds(off, rows), :].astype(F32))
        for h in range(ML_HEADS):
            sl = slice(h * ML_V, (h + 1) * ML_V)
            x = hb[:, sl]
            y = x * lax.rsqrt(jnp.mean(x * x, axis=-1, keepdims=True) + EPS) * hn[:, sl]
            y_ref[pl.ds(off, rows), sl] = (y * gate[:, sl]).astype(y_ref.dtype)
        return carry

    lax.fori_loop(0, t // rows, epilogue, 0)


def _mlstm(qk, v, og, g_col, g_row, hn):
    b, t, _ = qk.shape
    nc = t // ML_CHUNK
    seq = lambda i: (i, 0, 0)
    return pl.pallas_call(
        _mlstm_kernel,
        out_shape=jax.ShapeDtypeStruct((b, t, ML_V_W), BF),
        grid=(b,),
        in_specs=[
            pl.BlockSpec((None, t, 2 * ML_QK_W), seq),
            pl.BlockSpec((None, t, ML_V_W), seq),
            pl.BlockSpec((None, t, ML_V_W), seq),
            pl.BlockSpec((None, t, LANES), seq),
            pl.BlockSpec((None, nc, PAIR_ROWS, LANES), lambda i: (i, 0, 0, 0)),
            pl.BlockSpec((1, ML_V_W), lambda i: (0, 0)),
        ],
        out_specs=pl.BlockSpec((None, t, ML_V_W), seq),
        scratch_shapes=[
            pltpu.VMEM((2, t, ML_V_W), F32),
            pltpu.VMEM((t, LANES), F32),
            pltpu.VMEM((t, LANES), F32),
            pltpu.VMEM((t, LANES), F32),
            pltpu.VMEM((nc, 8, LANES), F32),
        ],
        compiler_params=pltpu.CompilerParams(
            dimension_semantics=("parallel",), vmem_limit_bytes=VMEM_LIMIT),
    )(qk, v, og, g_col, g_row, hn)


_GD_QKV_W = 2 * GD_K_W + GD_V_W
_GD_HALO = 8
_GD_PAIRS = GD_HEADS // 2
_GD_PD = 2 * _GD_PAIRS
_GD_ROW_A = 8
_GD_A_CHUNKS = 2


def _gdn_kernel(c_ref, z_ref, gc_ref, gr_ref, cw_ref, acol_ref, arow_ref, hn_ref, y_ref,
                xp_ref, qkv_ref, acc_ref, u_ref, wq_ref, att_ref, kd_ref, egl_ref):
    t = c_ref.shape[0]
    L = GD_CHUNK
    nc = t // L
    incl_lo, incl_up, _, _ = _tri_masks(L)
    lower_b = jnp.where(incl_lo, 1.0, 0.0).astype(BF)
    upper_b = jnp.where(incl_up, 1.0, 0.0).astype(BF)
    gsum = _group_ones(LANES, GD_K)

    xp_ref[0:_GD_HALO, :] = jnp.zeros((_GD_HALO, _GD_QKV_W), F32)
    xp_ref[t + _GD_HALO:t + 2 * _GD_HALO, :] = jnp.zeros((_GD_HALO, _GD_QKV_W), F32)
    xp_ref[_GD_HALO:t + _GD_HALO, :] = c_ref[...].astype(F32)
    cw = cw_ref[...]
    rows = 256
    pad = (GD_CONV - 1) // 2

    def conv_tile(j, carry):
        off = pl.multiple_of(j * rows, rows)
        xt = xp_ref[pl.ds(off, rows + 2 * _GD_HALO), :]
        y = jnp.zeros((rows, _GD_QKV_W), F32)
        for tap in range(GD_CONV):
            s0 = _GD_HALO + tap - pad
            y = y + cw[tap:tap + 1, :] * xt[s0:s0 + rows, :]
        y = y * jax.nn.sigmoid(y)
        for s in range(2 * GD_K_W // LANES):
            sl = slice(s * LANES, (s + 1) * LANES)
            ys = y[:, sl]
            inv = lax.rsqrt(_sel_dot_lhs(ys * ys, gsum, parts=2) + EPS)
            scale = GD_K ** -0.5 if s < GD_K_W // LANES else 1.0
            qkv_ref[pl.ds(off, rows), sl] = ys * inv * scale
        qkv_ref[pl.ds(off, rows), 2 * GD_K_W:] = y[:, 2 * GD_K_W:]
        return carry

    lax.fori_loop(0, t // rows, conv_tile, 0, unroll=2)

    tt = lax.broadcasted_iota(jnp.int32, (L, LANES), 0)
    ln = lax.broadcasted_iota(jnp.int32, (L, LANES), 1)
    ss = ln % GD_K
    lane_lo = ln < GD_K
    incl = (ss <= tt, ss >= tt)
    strict = (ss < tt, ss > tt)
    eye_pair = jnp.where(ss == tt, 1.0, 0.0)
    bw_cols = (ln // GD_HEADS) % 2 == 1
    r2 = lax.broadcasted_iota(jnp.int32, (2 * L, LANES), 0)
    c2 = lax.broadcasted_iota(jnp.int32, (2 * L, LANES), 1)
    same_half = (r2 // L) == (c2 // GD_K)
    bd_upper = jnp.where(same_half & (c2 % GD_K >= r2 % L), 1.0, 0.0).astype(BF)
    bd_lower = jnp.where(same_half & (c2 % GD_K <= r2 % L), 1.0, 0.0).astype(BF)
    ej = lax.broadcasted_iota(jnp.int32, (LANES, _GD_PD * LANES), 0)
    en = lax.broadcasted_iota(jnp.int32, (LANES, _GD_PD * LANES), 1) // GD_K
    e_a = jnp.where(ej == GATE_A + en, 1.0, 0.0).astype(BF)
    e_b = jnp.where(ej == GATE_B + en, 1.0, 0.0).astype(BF)
    a_col = jnp.exp(acol_ref[...])
    a_row = jnp.exp(arow_ref[...])

    def phase_a(j, carry):
        chains = []
        for ci in range(_GD_A_CHUNKS):
            cidx = j * _GD_A_CHUNKS + ci
            off = pl.multiple_of(cidx * L, L)
            gcol = gc_ref[pl.ds(off, L), :]
            grow = gr_ref[cidx]
            qkv = qkv_ref[pl.ds(off, L), :]
            g_c = -a_col * _softplus(gcol)
            cum_c = jnp.where(bw_cols, _sel_dot_rhs(upper_b, g_c), _sel_dot_rhs(lower_b, g_c))
            gcb_all = _sel_dot_lhs(cum_c, e_a, parts=2)
            beb_all = _sel_dot_lhs(jax.nn.sigmoid(gcol), e_b, parts=1)
            g_r = -a_row * _softplus(grow)
            cum_r = (_sel_dot_lhs(g_r, bd_upper), _sel_dot_lhs(g_r, bd_lower))
            for p in range(_GD_PAIRS):
                q_p = qkv[:, p * LANES:(p + 1) * LANES]
                k_p = qkv[:, GD_K_W + p * LANES:GD_K_W + (p + 1) * LANES]
                v_p = qkv[:, 2 * GD_K_W + p * LANES:2 * GD_K_W + (p + 1) * LANES]
                kexp = _expand_pair(k_p.astype(BF), lane_lo)
                kk = _dot_nt(k_p.astype(BF), kexp)
                qk = _dot_nt(q_p.astype(BF), kexp)
                for d in range(2):
                    pd = d * _GD_PAIRS + p
                    gcb = gcb_all[:, pd * LANES:(pd + 1) * LANES]
                    beb = beb_all[:, pd * LANES:(pd + 1) * LANES]
                    gcr = cum_r[d][_GD_ROW_A + pd:_GD_ROW_A + pd + 1, :]
                    decay = jnp.exp(jnp.where(incl[d], gcb - gcr, -jnp.inf))
                    lmat = jnp.where(strict[d], beb * kk * decay, 0.0)
                    eg = jnp.exp(gcb)
                    gl = gcb[L - 1:L, :] if d == 0 else gcb[0:1, :]
                    x = jnp.concatenate([v_p * beb, k_p * (beb * eg)], axis=1)
                    att_ref[pd, pl.ds(off, L), :] = (qk * decay).astype(BF)
                    wq_ref[pd, cidx, L:2 * L, :] = (q_p * eg).astype(BF)
                    kd_ref[pd, pl.ds(off, L), :] = (k_p * jnp.exp(gl - gcb)).astype(BF)
                    egl_ref[cidx, pd] = jnp.broadcast_to(jnp.exp(gl), (8, LANES))
                    chains.append(dict(pd=pd, cidx=cidx, off=off, p=lmat.astype(BF), l=lmat, x=x))

        for ch in chains:
            ch["t"] = eye_pair - ch["l"]
            ch["p"] = _dot(ch["p"], _expand_pair(ch["p"], lane_lo)).astype(BF)
        power = 2
        while power < L:
            last = 2 * power >= L
            for ch in chains:
                rhs = _expand_pair(ch["t"].astype(BF), lane_lo)
                if not last:
                    rhs = jnp.concatenate([_expand_pair(ch["p"], lane_lo), rhs], axis=1)
                ch["y"] = _dot(ch["p"], rhs)
            for ch in chains:
                if not last:
                    ch["p"] = ch["y"][:, :LANES].astype(BF)
                ch["t"] = ch["t"] + ch["y"][:, -LANES:]
            power *= 2
        for ch in chains:
            xb = ch["x"].astype(BF)
            rhs = jnp.concatenate([_expand_pair(xb[:, :LANES], lane_lo),
                                   _expand_pair(xb[:, LANES:], lane_lo)], axis=1)
            sol = _dot(ch["t"].astype(BF), rhs)
            u_ref[ch["pd"], pl.ds(ch["off"], L), :] = sol[:, :LANES]
            wq_ref[ch["pd"], ch["cidx"], 0:L, :] = sol[:, LANES:].astype(BF)
        return carry

    lax.fori_loop(0, nc // _GD_A_CHUNKS, phase_a, 0, unroll=2)

    rb = lax.broadcasted_iota(jnp.int32, (LANES, LANES), 0) // GD_K
    cb = lax.broadcasted_iota(jnp.int32, (LANES, LANES), 1) // GD_V
    blockdiag = rb == cb

    def phase_b(i, states):
        items = []
        for d in range(2):
            cidx = i if d == 0 else nc - 1 - i
            off = pl.multiple_of(cidx * L, L)
            for p in range(_GD_PAIRS):
                pd = d * _GD_PAIRS + p
                items.append((pd, p, cidx, off))
        s_b = [s.astype(BF) for s in states]
        res = [_dot(wq_ref[pd, cidx], s_b[pd]) for pd, _, cidx, off in items]
        v_new = [(u_ref[pd, pl.ds(off, L), :] - res[pd][:L]).astype(BF) for pd, _, _, off in items]
        new_states = []
        for pd, p, cidx, off in items:
            upd = _dot_tn(kd_ref[pd, pl.ds(off, L), :], v_new[pd])
            new_states.append(states[pd] * egl_ref[cidx, pd][0:1, :] + jnp.where(blockdiag, upd, 0.0))
        for pd, p, cidx, off in items:
            o = res[pd][L:] + _dot(att_ref[pd, pl.ds(off, L), :], _expand_pair(v_new[pd], lane_lo))
            acc_ref[pd // _GD_PAIRS, pl.ds(off, L), p * LANES:(p + 1) * LANES] = o
        return tuple(new_states)

    init = tuple(jnp.zeros((LANES, LANES), F32) for _ in range(_GD_PD))
    lax.fori_loop(0, nc, phase_b, init, unroll=2)

    hn = hn_ref[...]
    gsum_v = _group_ones(GD_V_W, GD_V)

    def epilogue(j, carry):
        off = pl.multiple_of(j * rows, rows)
        x = acc_ref[0, pl.ds(off, rows), :] + acc_ref[1, pl.ds(off, rows), :]
        z = z_ref[pl.ds(off, rows), :].astype(F32)
        ms = _sel_dot_lhs(x * x, gsum_v) * (1.0 / GD_V)
        y = x * lax.rsqrt(ms + EPS) * hn
        y_ref[pl.ds(off, rows), :] = (y * (z * jax.nn.sigmoid(z))).astype(y_ref.dtype)
        return carry

    lax.fori_loop(0, t // rows, epilogue, 0)


def _gdn(c_qkv, z, g_col, g_rowp, conv_w, a_col, a_rowp, hn):
    b, t, _ = c_qkv.shape
    nc = t // GD_CHUNK
    seq = lambda i: (i, 0, 0)
    const = lambda i: (0, 0)
    return pl.pallas_call(
        _gdn_kernel,
        out_shape=jax.ShapeDtypeStruct((b, t, GD_V_W), BF),
        grid=(b,),
        in_specs=[
            pl.BlockSpec((None, t, _GD_QKV_W), seq),
            pl.BlockSpec((None, t, GD_V_W), seq),
            pl.BlockSpec((None, t, LANES), seq),
            pl.BlockSpec((None, nc, PAIR_ROWS, LANES), lambda i: (i, 0, 0, 0)),
            pl.BlockSpec((GD_CONV, _GD_QKV_W), const),
            pl.BlockSpec((1, LANES), const),
            pl.BlockSpec((PAIR_ROWS, LANES), const),
            pl.BlockSpec((1, GD_V_W), const),
        ],
        out_specs=pl.BlockSpec((None, t, GD_V_W), seq),
        scratch_shapes=[
            pltpu.VMEM((t + 2 * _GD_HALO, _GD_QKV_W), F32),
            pltpu.VMEM((t, _GD_QKV_W), F32),
            pltpu.VMEM((2, t, GD_V_W), F32),
            pltpu.VMEM((_GD_PD, t, LANES), F32),
            pltpu.VMEM((_GD_PD, nc, 2 * GD_CHUNK, LANES), BF),
            pltpu.VMEM((_GD_PD, t, LANES), BF),
            pltpu.VMEM((_GD_PD, t, LANES), BF),
            pltpu.VMEM((nc, _GD_PD, 8, LANES), F32),
        ],
        compiler_params=pltpu.CompilerParams(
            dimension_semantics=("parallel",), vmem_limit_bytes=VMEM_LIMIT),
    )(c_qkv, z, g_col, g_rowp, conv_w, a_col, a_rowp, hn)


def _outproj_kernel(x_ref, ya_ref, yb_ref, yc_ref, w_ref, o_ref):
    acc = x_ref[...]
    acc = acc + _dot(ya_ref[...], w_ref[0:GM_WIDTH, :])
    acc = acc + _dot(yb_ref[...], w_ref[GM_WIDTH:GM_WIDTH + ML_V_W, :])
    acc = acc + _dot(yc_ref[...], w_ref[GM_WIDTH + ML_V_W:, :])
    o_ref[...] = acc


def _outproj(x2, ya, yb, yc, w, tm):
    n = x2.shape[0]
    row = lambda i: (i, 0)
    return pl.pallas_call(
        _outproj_kernel,
        out_shape=jax.ShapeDtypeStruct((n, D_MODEL), F32),
        grid=(n // tm,),
        in_specs=[
            pl.BlockSpec((tm, D_MODEL), row),
            pl.BlockSpec((tm, GM_WIDTH), row),
            pl.BlockSpec((tm, ML_V_W), row),
            pl.BlockSpec((tm, GD_V_W), row),
            pl.BlockSpec((D_MODEL, D_MODEL), lambda i: (0, 0)),
        ],
        out_specs=pl.BlockSpec((tm, D_MODEL), row),
        compiler_params=pltpu.CompilerParams(
            dimension_semantics=("parallel",), vmem_limit_bytes=VMEM_LIMIT),
    )(x2, ya, yb, yc, w)


_FFN_HALO = 16
_FFN_CHUNK = 256
_FFN_DOWN_GROUP = 4
_FFN_SLOTS = 3


def _ffn_kernel(xm_ref, xp_ref, xn_ref, g_ref, wup_ref, cw_ref, cb_ref, wdn_ref, gf_ref, o_ref,
                h_ref, act_ref, up_ref, *, tiles_per_seq, final_norm):
    i = pl.program_id(0)
    tm = xm_ref.shape[0]
    g = g_ref[...]
    pos = i % tiles_per_seq

    def norm(x):
        return x * lax.rsqrt(jnp.mean(x * x, axis=-1, keepdims=True) + EPS) * g

    hp = jnp.where(pos == 0, 0.0, norm(xp_ref[...]))
    hn = jnp.where(pos == tiles_per_seq - 1, 0.0, norm(xn_ref[...]))
    xm = xm_ref[...]
    h_ref[0:_FFN_HALO, :] = hp.astype(BF)
    h_ref[_FFN_HALO:_FFN_HALO + tm, :] = norm(xm).astype(BF)
    h_ref[_FFN_HALO + tm:, :] = hn.astype(BF)
    hh = h_ref[...]
    cw = cw_ref[...]
    cb = cb_ref[...]
    o_ref[...] = xm

    def up_project(j):
        c0 = j * _FFN_CHUNK
        slot = j % _FFN_SLOTS
        up_ref[slot, :, 0:_FFN_CHUNK] = _dot(hh, wup_ref[:, c0:c0 + _FFN_CHUNK])
        up_ref[slot, :, _FFN_CHUNK:] = _dot(hh, wup_ref[:, D_FF + c0:D_FF + c0 + _FFN_CHUNK])

    def conv(slot, lanes, c0):
        sl = slice(c0, c0 + _FFN_CHUNK)
        return (cw[0:1, sl] * up_ref[slot, _FFN_HALO - 1:_FFN_HALO - 1 + tm, lanes]
                + cw[1:2, sl] * up_ref[slot, _FFN_HALO:_FFN_HALO + tm, lanes]
                + cw[2:3, sl] * up_ref[slot, _FFN_HALO + 1:_FFN_HALO + 1 + tm, lanes] + cb[:, sl])

    n_chunks = D_FF // _FFN_CHUNK
    for j in range(_FFN_SLOTS - 1):
        up_project(j)
    for j in range(n_chunks):
        if j + _FFN_SLOTS - 1 < n_chunks:
            up_project(j + _FFN_SLOTS - 1)
        c0 = j * _FFN_CHUNK
        gt = conv(j % _FFN_SLOTS, slice(0, _FFN_CHUNK), c0)
        vl = conv(j % _FFN_SLOTS, slice(_FFN_CHUNK, 2 * _FFN_CHUNK), D_FF + c0)
        act_ref[:, c0:c0 + _FFN_CHUNK] = (gt * jax.nn.sigmoid(gt) * vl).astype(BF)
        if (j + 1) % _FFN_DOWN_GROUP == 0 or j + 1 == n_chunks:
            k0 = (j // _FFN_DOWN_GROUP) * _FFN_DOWN_GROUP * _FFN_CHUNK
            k1 = c0 + _FFN_CHUNK
            o_ref[...] += _dot(act_ref[:, k0:k1], wdn_ref[k0:k1, :])

    if final_norm:
        y = o_ref[...]
        o_ref[...] = y * lax.rsqrt(jnp.mean(y * y, axis=-1, keepdims=True) + EPS) * gf_ref[...]


def _ffn(x2, gain, w_up, conv_w, conv_b, w_down, gain_final, seq_len, tm, final_norm):
    n = x2.shape[0]
    hb = tm // _FFN_HALO
    nhb = n // _FFN_HALO
    row = lambda i: (i, 0)
    const = lambda i: (0, 0)
    kern = functools.partial(_ffn_kernel, tiles_per_seq=seq_len // tm, final_norm=final_norm)
    return pl.pallas_call(
        kern,
        out_shape=jax.ShapeDtypeStruct((n, D_MODEL), F32),
        grid=(n // tm,),
        in_specs=[
            pl.BlockSpec((tm, D_MODEL), row),
            pl.BlockSpec((_FFN_HALO, D_MODEL), lambda i: (jnp.maximum(i * hb - 1, 0), 0)),
            pl.BlockSpec((_FFN_HALO, D_MODEL), lambda i: (jnp.minimum((i + 1) * hb, nhb - 1), 0)),
            pl.BlockSpec((1, D_MODEL), const),
            pl.BlockSpec((D_MODEL, 2 * D_FF), const, pipeline_mode=pl.Buffered(1)),
            pl.BlockSpec((FFN_CONV, 2 * D_FF), const),
            pl.BlockSpec((1, 2 * D_FF), const),
            pl.BlockSpec((D_FF, D_MODEL), const, pipeline_mode=pl.Buffered(1)),
            pl.BlockSpec((1, D_MODEL), const),
        ],
        out_specs=pl.BlockSpec((tm, D_MODEL), row),
        scratch_shapes=[
            pltpu.VMEM((tm + 2 * _FFN_HALO, D_MODEL), BF),
            pltpu.VMEM((tm, D_FF), BF),
            pltpu.VMEM((_FFN_SLOTS, tm + 2 * _FFN_HALO, 2 * _FFN_CHUNK), F32),
        ],
        compiler_params=pltpu.CompilerParams(
            dimension_semantics=("parallel",), vmem_limit_bytes=VMEM_LIMIT),
    )(x2, x2, x2, gain, w_up, conv_w, conv_b, w_down, gain_final)


_TM = 512
_TM_FFN = 1024


def _layer(x2, batch, seq_len, norm_mix, w_in, gm_norm, gm_ws, gm_bs, ml_gate_bias, ml_head_norm,
           gd_conv, gd_a_log, gd_dt_bias, gd_head_norm, w_out, norm_ffn, w_up, ffn_conv, ffn_conv_b,
           w_down, norm_final, final_norm):
    n = x2.shape[0]
    nc = seq_len // GD_CHUNK
    g0 = 2 * GM_WIDTH + 2 * ML_QK_W + 2 * ML_V_W
    g1 = g0 + 4 * ML_HEADS
    c1 = g1 + _GD_QKV_W + GD_V_W
    q0 = 2 * GM_WIDTH
    col_scale = jnp.ones((_IN_MAIN,), F32).at[q0:q0 + ML_QK_W].set(ML_QK ** -0.5)
    w_main = (jnp.concatenate([w_in[:, :g0], w_in[:, g1:c1]], axis=1) * col_scale).astype(BF)
    w_gate = jnp.concatenate(
        [w_in[:, g0:g1], w_in[:, c1:], jnp.zeros((D_MODEL, LANES - N_GATES), F32)], axis=1).astype(BF)
    gate_bias = jnp.concatenate(
        [ml_gate_bias, gd_dt_bias.reshape(-1), jnp.zeros((LANES - GATE_B,), F32)]).reshape(1, LANES)
    uv, qk, v, og, cq, z, gates = _inproj(x2, norm_mix.reshape(1, D_MODEL), w_main, w_gate, gate_bias, _TM)

    seq = lambda a: a.reshape(batch, seq_len, a.shape[-1])
    g_col = seq(gates)
    g32 = g_col[:, :, :N_GATES]
    g_rowp = g32.reshape(batch, nc, GD_CHUNK, 4, 2, _GD_PAIRS, 2).transpose(0, 1, 3, 4, 5, 6, 2)
    g_rowp = g_rowp.reshape(batch, nc, PAIR_ROWS, LANES)

    ws_cat = gm_ws.transpose(1, 0, 2).reshape(GM_CHUNK, GM_GROUPS * GM_CHUNK).astype(BF)
    bs_exp = jnp.repeat(gm_bs.T, GM_DIM, axis=1)
    ya = _gmlp(seq(uv), gm_norm.reshape(1, GM_WIDTH), ws_cat, bs_exp)

    yb = _mlstm(seq(qk), seq(v), seq(og), g_col, g_rowp, ml_head_norm.reshape(1, ML_V_W))

    a_flat = gd_a_log.reshape(-1)
    a_col = jnp.zeros((LANES,), F32).at[GATE_A:GATE_B].set(a_flat).reshape(1, LANES)
    a_rowp = jnp.zeros((PAIR_ROWS, LANES), F32).at[_GD_ROW_A:_GD_ROW_A + _GD_PD].set(
        jnp.repeat(a_flat.reshape(_GD_PD, 2), GD_K, axis=1))
    yc = _gdn(seq(cq), seq(z), g_col, g_rowp, gd_conv, a_col, a_rowp, gd_head_norm.reshape(1, GD_V_W))

    flat = lambda a: a.reshape(n, a.shape[-1])
    x2 = _outproj(x2, flat(ya), flat(yb), flat(yc), w_out.astype(BF), _TM)
    return _ffn(x2, norm_ffn.reshape(1, D_MODEL), w_up.astype(BF), ffn_conv, ffn_conv_b.reshape(1, -1),
                w_down.astype(BF), norm_final.reshape(1, D_MODEL), seq_len, _TM_FFN, final_norm)


def kernel(x, norm_mix, w_in, gm_norm, gm_ws, gm_bs, ml_gate_bias, ml_head_norm, gd_conv, gd_A_log,
           gd_dt_bias, gd_head_norm, w_out, norm_ffn, w_up, ffn_conv, ffn_conv_b, w_down, norm_final):
    batch, seq_len, _ = x.shape
    depth = w_in.shape[0]
    x2 = x.reshape(batch * seq_len, D_MODEL)
    for i in range(depth):
        x2 = _layer(x2, batch, seq_len, norm_mix[i], w_in[i], gm_norm[i], gm_ws[i], gm_bs[i],
                    ml_gate_bias[i], ml_head_norm[i], gd_conv[i], gd_A_log[i], gd_dt_bias[i],
                    gd_head_norm[i], w_out[i], norm_ffn[i], w_up[i], ffn_conv[i], ffn_conv_b[i],
                    w_down[i], norm_final, i == depth - 1)
    return x2.reshape(batch, seq_len, D_MODEL)
```

```python
import functools

import jax
import jax.numpy as jnp
from jax import lax
from jax.experimental import pallas as pl
from jax.experimental.pallas import tpu as pltpu

D_MODEL = 1024
GM_GROUPS = 4
GM_DIM = 64
GM_WIDTH = GM_GROUPS * GM_DIM
GM_CHUNK = 128
ML_HEADS = 4
ML_QK = 64
ML_V = 128
ML_QK_W = ML_HEADS * ML_QK
ML_V_W = ML_HEADS * ML_V
ML_CHUNK = 64
GD_HEADS = 4
GD_K = 64
GD_V = 64
GD_K_W = GD_HEADS * GD_K
GD_V_W = GD_HEADS * GD_V
GD_CHUNK = 64
GD_CONV = 5
D_FF = 2816
FFN_CONV = 3
EPS = 1e-6

LANES = 128
N_GATES = 32
GATE_I, GATE_F, GATE_A, GATE_B = 0, 8, 16, 24
PAIR_ROWS = 16
VMEM_LIMIT = 56 * 1024 * 1024

BF = jnp.bfloat16
F32 = jnp.float32


def _dot(a, b):
    return jnp.dot(a, b, preferred_element_type=F32)


def _dot_nt(a, b):
    return lax.dot_general(a, b, (((1,), (1,)), ((), ())), preferred_element_type=F32)


def _dot_tn(a, b):
    return lax.dot_general(a, b, (((0,), (0,)), ((), ())), preferred_element_type=F32)


def _split(x, parts):
    out = []
    for _ in range(parts - 1):
        hi = x.astype(BF)
        out.append(hi)
        x = x - hi.astype(F32)
    out.append(x.astype(BF))
    return out


def _sel_dot_rhs(sel, x, parts=3):
    acc = None
    for term in _split(x, parts):
        y = _dot(sel, term)
        acc = y if acc is None else acc + y
    return acc


def _sel_dot_lhs(x, sel, parts=3):
    acc = None
    for term in _split(x, parts):
        y = _dot(term, sel)
        acc = y if acc is None else acc + y
    return acc


def _expand_pair(xb, lane_lo):
    z = jnp.zeros((), xb.dtype)
    return jnp.concatenate([jnp.where(lane_lo, xb, z), jnp.where(lane_lo, z, xb)], axis=0)


def _run_interleaved(gens):
    results = [None] * len(gens)
    pending = list(range(len(gens)))
    while pending:
        for k in list(pending):
            try:
                next(gens[k])
            except StopIteration as stop:
                results[k] = stop.value
                pending.remove(k)
    return tuple(results)


def _running_max(x, reverse):
    n, w = x.shape
    sh = 1
    while sh < n:
        fill = jnp.full((sh, w), -jnp.inf, x.dtype)
        if reverse:
            shifted = jnp.concatenate([x[sh:], fill], axis=0)
        else:
            shifted = jnp.concatenate([fill, x[:n - sh]], axis=0)
        x = jnp.maximum(x, shifted)
        sh *= 2
    return x


def _pair_cumsum_mats():
    r = lax.broadcasted_iota(jnp.int32, (LANES, LANES), 0)
    c = lax.broadcasted_iota(jnp.int32, (LANES, LANES), 1)
    half = LANES // 2
    same = (r // half) == (c // half)
    prefix = jnp.where(same & (c >= r), 1.0, 0.0).astype(BF)
    suffix = jnp.where(same & (c <= r), 1.0, 0.0).astype(BF)
    return prefix, suffix


def _column_spread(first_col, n_cols, width):
    j = lax.broadcasted_iota(jnp.int32, (LANES, n_cols * width), 0)
    n = lax.broadcasted_iota(jnp.int32, (LANES, n_cols * width), 1) // width
    return jnp.where(j == first_col + n, 1.0, 0.0).astype(BF)


def _softplus(x):
    return jnp.maximum(x, 0.0) + jnp.log1p(jnp.exp(-jnp.abs(x)))


def _log_sigmoid(x):
    return jnp.minimum(x, 0.0) - jnp.log1p(jnp.exp(-jnp.abs(x)))


def _tri_masks(n):
    r = lax.broadcasted_iota(jnp.int32, (n, n), 0)
    c = lax.broadcasted_iota(jnp.int32, (n, n), 1)
    return c <= r, c >= r, c < r, c > r


def _group_ones(width, group):
    r = lax.broadcasted_iota(jnp.int32, (width, width), 0) // group
    c = lax.broadcasted_iota(jnp.int32, (width, width), 1) // group
    return jnp.where(r == c, 1.0, 0.0).astype(BF)


_IN_SEGS = ((0, 512), (512, 1024), (1024, 1536), (1536, 2048), (2048, 2816), (2816, 3072))
_IN_MAIN = 3072


_IN_HALO = 16
_IN_CONV_SEG = 4


def _inproj_kernel(x_ref, xp_ref, xn_ref, g_ref, w_ref, wg_ref, b_ref, cw_ref,
                   o_uv, o_qk, o_v, o_o, o_c, o_z, o_g, h_ref, c_ref, *, tiles_per_seq):
    tm = x_ref.shape[0]
    g = g_ref[...]
    pos = pl.program_id(0) % tiles_per_seq

    def norm(x):
        return x * lax.rsqrt(jnp.mean(x * x, axis=-1, keepdims=True) + EPS) * g

    h_ref[0:_IN_HALO, :] = jnp.where(pos == 0, 0.0, norm(xp_ref[...])).astype(BF)
    h_ref[_IN_HALO:_IN_HALO + tm, :] = norm(x_ref[...]).astype(BF)
    h_ref[_IN_HALO + tm:, :] = jnp.where(pos == tiles_per_seq - 1, 0.0, norm(xn_ref[...])).astype(BF)
    lo, hi = _IN_SEGS[_IN_CONV_SEG]
    c_ref[...] = _dot(h_ref[...], w_ref[:, lo:hi])
    h = h_ref[_IN_HALO:_IN_HALO + tm, :]
    for idx, o in enumerate((o_uv, o_qk, o_v, o_o, o_c, o_z)):
        if idx != _IN_CONV_SEG:
            lo, hi = _IN_SEGS[idx]
            o[...] = _dot(h, w_ref[:, lo:hi]).astype(o.dtype)
    o_g[...] = _dot(h, wg_ref[...]) + b_ref[...]

    cw = cw_ref[...]
    gsum = _group_ones(LANES, GD_K)
    pad = (GD_CONV - 1) // 2
    for s in range(_GD_QKV_W // LANES):
        sl = slice(s * LANES, (s + 1) * LANES)
        y = jnp.zeros((tm, LANES), F32)
        for tap in range(GD_CONV):
            r0 = _IN_HALO + tap - pad
            y = y + cw[tap:tap + 1, sl] * c_ref[r0:r0 + tm, sl]
        y = y * jax.nn.sigmoid(y)
        if s < 2 * GD_K_W // LANES:
            scale = GD_K ** -0.5 if s < GD_K_W // LANES else 1.0
            y = y * (lax.rsqrt(_sel_dot_lhs(y * y, gsum, parts=2) + EPS) * scale)
        o_c[:, sl] = y.astype(o_c.dtype)


def _layer_spec(shape, layer, **kw):
    zeros = (0,) * len(shape)
    return pl.BlockSpec((None,) + tuple(shape), lambda i: (layer,) + zeros, **kw)


def _inproj(x2, gain, w_main, w_gate, gate_bias, conv_w, layer, seq_len, tm):
    n = x2.shape[0]
    widths = [hi - lo for lo, hi in _IN_SEGS]
    out_shape = [jax.ShapeDtypeStruct((n, w), BF) for w in widths]
    out_shape.append(jax.ShapeDtypeStruct((n, LANES), F32))
    hb = tm // _IN_HALO
    nhb = n // _IN_HALO
    row = lambda i: (i, 0)
    const = lambda i: (0, 0)
    return pl.pallas_call(
        functools.partial(_inproj_kernel, tiles_per_seq=seq_len // tm),
        out_shape=out_shape,
        grid=(n // tm,),
        in_specs=[
            pl.BlockSpec((tm, D_MODEL), row),
            pl.BlockSpec((_IN_HALO, D_MODEL), lambda i: (jnp.maximum(i * hb - 1, 0), 0)),
            pl.BlockSpec((_IN_HALO, D_MODEL), lambda i: (jnp.minimum((i + 1) * hb, nhb - 1), 0)),
            pl.BlockSpec((1, D_MODEL), const),
            _layer_spec((D_MODEL, _IN_MAIN), layer),
            _layer_spec((D_MODEL, LANES), layer),
            pl.BlockSpec((1, LANES), const),
            pl.BlockSpec((GD_CONV, _GD_QKV_W), const),
        ],
        out_specs=[pl.BlockSpec((tm, w), row) for w in widths] + [pl.BlockSpec((tm, LANES), row)],
        scratch_shapes=[
            pltpu.VMEM((tm + 2 * _IN_HALO, D_MODEL), BF),
            pltpu.VMEM((tm + 2 * _IN_HALO, _GD_QKV_W), F32),
        ],
        compiler_params=pltpu.CompilerParams(
            dimension_semantics=("parallel",), vmem_limit_bytes=VMEM_LIMIT),
    )(x2, x2, x2, gain, w_main, w_gate, gate_bias, conv_w)


def _gmlp_kernel(uv_ref, gn_ref, ws_ref, bs_ref, y_ref):
    t = uv_ref.shape[0]
    gsum = _group_ones(GM_WIDTH, GM_DIM)
    rr = lax.broadcasted_iota(jnp.int32, (GM_GROUPS * GM_CHUNK, GM_WIDTH), 0) // GM_CHUNK
    cc = lax.broadcasted_iota(jnp.int32, (GM_GROUPS * GM_CHUNK, GM_WIDTH), 1) // GM_DIM
    blockdiag = rr == cc
    ws = ws_ref[...]
    bs = bs_ref[...]
    gn = gn_ref[...]

    def body(c, carry):
        off = pl.multiple_of(c * GM_CHUNK, GM_CHUNK)
        uv = uv_ref[pl.ds(off, GM_CHUNK), :].astype(F32)
        u = jax.nn.gelu(uv[:, :GM_WIDTH])
        v = jax.nn.gelu(uv[:, GM_WIDTH:])
        ms = _sel_dot_lhs(v * v, gsum, parts=2) * (1.0 / GM_DIM)
        vn = (v * lax.rsqrt(ms + EPS) * gn).astype(BF)
        vexp = jnp.where(blockdiag, jnp.concatenate([vn] * GM_GROUPS, axis=0), jnp.zeros((), BF))
        sg = _dot(ws, vexp) + bs
        y_ref[pl.ds(off, GM_CHUNK), :] = (u * sg).astype(y_ref.dtype)
        return carry

    lax.fori_loop(0, t // GM_CHUNK, body, 0, unroll=4)


def _gmlp(uv, gn, ws_cat, bs_exp):
    b, t, _ = uv.shape
    const = lambda i: (0, 0)
    return pl.pallas_call(
        _gmlp_kernel,
        out_shape=jax.ShapeDtypeStruct((b, t, GM_WIDTH), BF),
        grid=(b,),
        in_specs=[
            pl.BlockSpec((None, t, 2 * GM_WIDTH), lambda i: (i, 0, 0)),
            pl.BlockSpec((1, GM_WIDTH), const),
            pl.BlockSpec((GM_CHUNK, GM_GROUPS * GM_CHUNK), const),
            pl.BlockSpec((GM_CHUNK, GM_WIDTH), const),
        ],
        out_specs=pl.BlockSpec((None, t, GM_WIDTH), lambda i: (i, 0, 0)),
        compiler_params=pltpu.CompilerParams(
            dimension_semantics=("parallel",), vmem_limit_bytes=VMEM_LIMIT),
    )(uv, gn, ws_cat, bs_exp)


def _mlstm_kernel(qk_ref, v_ref, og_ref, gc_ref, gr_ref, hn_ref, y_ref,
                  acc_ref, bc_ref, cm_ref, wa_ref, st_ref):
    t = qk_ref.shape[0]
    L = ML_CHUNK
    nc = t // L
    lower, upper, _, _ = _tri_masks(L)
    lower_b = jnp.where(lower, 1.0, 0.0).astype(BF)
    upper_b = jnp.where(upper, 1.0, 0.0).astype(BF)
    ones_v = jnp.ones((L, ML_V), BF)
    zero_v = jnp.zeros((L, ML_V), BF)
    zero_c = jnp.zeros((ML_QK, ML_V), BF)
    tt = lax.broadcasted_iota(jnp.int32, (L, LANES), 0)
    ln = lax.broadcasted_iota(jnp.int32, (L, LANES), 1)
    lane_lo = ln < ML_QK
    mask = (ln % ML_QK <= tt, ln % ML_QK >= tt)
    f_cols = (ln >= GATE_F) & (ln < GATE_F + 2 * ML_HEADS)
    pair_prefix, pair_suffix = _pair_cumsum_mats()
    ro = lax.broadcasted_iota(jnp.int32, (LANES, 2 * ML_V), 0) // ML_QK
    co = lax.broadcasted_iota(jnp.int32, (LANES, 2 * ML_V), 1) // ML_V
    pair_rowsum = jnp.where(ro == co, 1.0, 0.0).astype(BF)
    bw_cols = (ln // ML_HEADS) % 2 == 1

    def gate_stats(c, carry):
        off = pl.multiple_of(c * L, L)
        gcol = gc_ref[pl.ds(off, L), :]
        lf = _log_sigmoid(gcol)
        bc = jnp.where(bw_cols, _sel_dot_rhs(upper_b, lf), _sel_dot_rhs(lower_b, lf))
        ig = pltpu.roll(gcol, GATE_F - GATE_I, 1)
        b_last = jnp.where(bw_cols[0:1], bc[0:1, :], bc[L - 1:L, :])
        a = b_last - bc + ig
        m_loc = jnp.max(a, axis=0, keepdims=True)
        r = ig - bc
        bc_ref[pl.ds(off, L), :] = bc
        cm_ref[pl.ds(off, L), :] = jnp.where(bw_cols, _running_max(r, True), _running_max(r, False))
        wa_ref[pl.ds(off, L), :] = jnp.exp(a - m_loc)
        st_ref[c] = jnp.concatenate([b_last, m_loc, jnp.zeros((6, LANES), F32)], axis=0)
        return carry

    lax.fori_loop(0, nc, gate_stats, 0, unroll=4)

    def chunk_dir(cidx, d, state):
        heads, m = state
        off = pl.multiple_of(cidx * L, L)
        qkc = qk_ref[pl.ds(off, L), :]
        vc = v_ref[pl.ds(off, L), :]
        grow = gr_ref[cidx]
        col0 = GATE_F + ML_HEADS * d
        spread_k = _column_spread(col0, ML_HEADS, ML_QK)
        spread_v = _column_spread(col0, ML_HEADS, ML_V)
        bc = bc_ref[pl.ds(off, L), :]
        wa = wa_ref[pl.ds(off, L), :]
        b_last = st_ref[cidx][0:1, :]
        m_loc = st_ref[cidx][1:2, :]
        mx = jnp.maximum(m, cm_ref[pl.ds(off, L), :])
        m_new = jnp.maximum(b_last + m, m_loc)
        f_rows = jnp.concatenate(
            [jnp.exp(b_last + m - m_new), jnp.exp(m_loc - m_new), jnp.zeros((6, LANES), F32)], axis=0)
        clean = lambda x: jnp.where(f_cols[:x.shape[0]], x, 0.0)
        mx_k = _sel_dot_lhs(clean(mx), spread_k, parts=2)
        wa_k = _sel_dot_lhs(clean(wa), spread_k, parts=1)
        w_inter = jnp.exp(_sel_dot_lhs(clean(m - mx), spread_v, parts=2))
        e_mt = jnp.exp(_sel_dot_lhs(clean(-(bc + mx)), spread_v, parts=2))
        f_v = _sel_dot_lhs(clean(f_rows), spread_v)
        cum_r = _sel_dot_lhs(_log_sigmoid(grow), pair_prefix if d == 0 else pair_suffix)
        yield
        new_heads = []
        outs = []
        for p in range(ML_HEADS // 2):
            row = 2 * d + p
            q_p = qkc[:, p * LANES:(p + 1) * LANES]
            k_p = qkc[:, ML_QK_W + p * LANES:ML_QK_W + (p + 1) * LANES]
            v0 = vc[:, 2 * p * ML_V:(2 * p + 1) * ML_V]
            v1 = vc[:, (2 * p + 1) * ML_V:(2 * p + 2) * ML_V]
            (c0, n0), (c1, n1) = heads[2 * p], heads[2 * p + 1]
            qk = _dot_nt(q_p, _expand_pair(k_p, lane_lo))
            rr = grow[row:row + 1, :] - cum_r[ML_HEADS + row:ML_HEADS + row + 1, :]
            s = qk * jnp.exp(jnp.where(mask[d], rr - mx_k[:, p * LANES:(p + 1) * LANES], -jnp.inf))
            rhs_s = jnp.concatenate(
                [jnp.concatenate([v0, zero_v], axis=1), jnp.concatenate([zero_v, v1], axis=1)], axis=0)
            intra = _dot(s.astype(BF), jnp.concatenate([rhs_s, pair_rowsum], axis=1))
            rhs_c = jnp.concatenate(
                [jnp.concatenate([c0.astype(BF), zero_c, n0.astype(BF), zero_c], axis=1),
                 jnp.concatenate([zero_c, c1.astype(BF), zero_c, n1.astype(BF)], axis=1)], axis=0)
            inter = _dot(q_p, rhs_c)
            w2 = w_inter[:, 2 * p * ML_V:(2 * p + 2) * ML_V]
            na = jnp.concatenate([w2, w2], axis=1) * inter + intra
            out = na[:, :2 * ML_V] / jnp.maximum(jnp.abs(na[:, 2 * ML_V:]), e_mt[:, 2 * p * ML_V:(2 * p + 2) * ML_V])
            outs.append(out)
            yield
            kw = (k_p.astype(F32) * wa_k[:, p * LANES:(p + 1) * LANES]).astype(BF)
            upd = _dot_tn(kw, jnp.concatenate([v0, v1, ones_v], axis=1))
            for hh, (c, n) in enumerate(((c0, n0), (c1, n1))):
                h = 2 * p + hh
                f_old = f_v[0:1, h * ML_V:(h + 1) * ML_V]
                f_loc = f_v[1:2, h * ML_V:(h + 1) * ML_V]
                rs = slice(hh * ML_QK, (hh + 1) * ML_QK)
                new_heads.append((f_old * c + f_loc * upd[rs, hh * ML_V:(hh + 1) * ML_V],
                                  f_old * n + f_loc * upd[rs, 2 * ML_V:3 * ML_V]))
            yield
        for p, out in enumerate(outs):
            acc_ref[d, pl.ds(off, L), 2 * p * ML_V:(2 * p + 2) * ML_V] = out
        return tuple(new_heads), m_new

    def body(i, carry):
        return _run_interleaved([chunk_dir(i, 0, carry[0]), chunk_dir(nc - 1 - i, 1, carry[1])])

    zero_state = jnp.zeros((ML_QK, ML_V), F32)
    init = (tuple((zero_state, zero_state) for _ in range(ML_HEADS)), jnp.zeros((1, LANES), F32))
    lax.fori_loop(0, nc, body, (init, init), unroll=4)

    hn = hn_ref[...]
    rows = 256

    def epilogue(j, carry):
        off = pl.multiple_of(j * rows, rows)
        hb = acc_ref[0, pl.ds(off, rows), :] + acc_ref[1, pl.ds(off, rows), :]
        gate = jax.nn.sigmoid(og_ref[pl.ds(off, rows), :].astype(F32))
        for h in range(ML_HEADS):
            sl = slice(h * ML_V, (h + 1) * ML_V)
            x = hb[:, sl]
            y = x * lax.rsqrt(jnp.mean(x * x, axis=-1, keepdims=True) + EPS) * hn[:, sl]
            y_ref[pl.ds(off, rows), sl] = (y * gate[:, sl]).astype(y_ref.dtype)
        return carry

    lax.fori_loop(0, t // rows, epilogue, 0, unroll=2)


def _mlstm(qk, v, og, g_col, g_row, hn):
    b, t, _ = qk.shape
    nc = t // ML_CHUNK
    seq = lambda i: (i, 0, 0)
    return pl.pallas_call(
        _mlstm_kernel,
        out_shape=jax.ShapeDtypeStruct((b, t, ML_V_W), BF),
        grid=(b,),
        in_specs=[
            pl.BlockSpec((None, t, 2 * ML_QK_W), seq),
            pl.BlockSpec((None, t, ML_V_W), seq),
            pl.BlockSpec((None, t, ML_V_W), seq),
            pl.BlockSpec((None, t, LANES), seq),
            pl.BlockSpec((None, nc, PAIR_ROWS, LANES), lambda i: (i, 0, 0, 0)),
            pl.BlockSpec((1, ML_V_W), lambda i: (0, 0)),
        ],
        out_specs=pl.BlockSpec((None, t, ML_V_W), seq),
        scratch_shapes=[
            pltpu.VMEM((2, t, ML_V_W), F32),
            pltpu.VMEM((t, LANES), F32),
            pltpu.VMEM((t, LANES), F32),
            pltpu.VMEM((t, LANES), F32),
            pltpu.VMEM((nc, 8, LANES), F32),
        ],
        compiler_params=pltpu.CompilerParams(
            dimension_semantics=("parallel",), vmem_limit_bytes=VMEM_LIMIT),
    )(qk, v, og, g_col, g_row, hn)


_GD_QKV_W = 2 * GD_K_W + GD_V_W
_GD_PAIRS = GD_HEADS // 2
_GD_PD = 2 * _GD_PAIRS
_GD_ROW_A = 8
_GD_A_CHUNKS = 2


def _gdn_kernel(qkv_ref, z_ref, gc_ref, gr_ref, acol_ref, arow_ref, hn_ref, y_ref,
                acc_ref, u_ref, wq_ref, att_ref, kd_ref, egl_ref):
    t = qkv_ref.shape[0]
    L = GD_CHUNK
    nc = t // L
    rows = 256
    incl_lo, incl_up, _, _ = _tri_masks(L)
    lower_b = jnp.where(incl_lo, 1.0, 0.0).astype(BF)
    upper_b = jnp.where(incl_up, 1.0, 0.0).astype(BF)

    tt = lax.broadcasted_iota(jnp.int32, (L, LANES), 0)
    ln = lax.broadcasted_iota(jnp.int32, (L, LANES), 1)
    ss = ln % GD_K
    lane_lo = ln < GD_K
    incl = (ss <= tt, ss >= tt)
    strict = (ss < tt, ss > tt)
    eye_pair = jnp.where(ss == tt, 1.0, 0.0)
    bw_cols = (ln // GD_HEADS) % 2 == 1
    r2 = lax.broadcasted_iota(jnp.int32, (2 * L, LANES), 0)
    c2 = lax.broadcasted_iota(jnp.int32, (2 * L, LANES), 1)
    same_half = (r2 // L) == (c2 // GD_K)
    bd_upper = jnp.where(same_half & (c2 % GD_K >= r2 % L), 1.0, 0.0).astype(BF)
    bd_lower = jnp.where(same_half & (c2 % GD_K <= r2 % L), 1.0, 0.0).astype(BF)
    ej = lax.broadcasted_iota(jnp.int32, (LANES, _GD_PD * LANES), 0)
    en = lax.broadcasted_iota(jnp.int32, (LANES, _GD_PD * LANES), 1) // GD_K
    e_a = jnp.where(ej == GATE_A + en, 1.0, 0.0).astype(BF)
    e_b = jnp.where(ej == GATE_B + en, 1.0, 0.0).astype(BF)
    a_col = jnp.exp(acol_ref[...])
    a_row = jnp.exp(arow_ref[...])

    def phase_a(j, carry):
        chains = []
        for ci in range(_GD_A_CHUNKS):
            cidx = j * _GD_A_CHUNKS + ci
            off = pl.multiple_of(cidx * L, L)
            gcol = gc_ref[pl.ds(off, L), :]
            grow = gr_ref[cidx]
            qkv = qkv_ref[pl.ds(off, L), :]
            g_c = -a_col * _softplus(gcol)
            cum_c = jnp.where(bw_cols, _sel_dot_rhs(upper_b, g_c), _sel_dot_rhs(lower_b, g_c))
            gcb_all = _sel_dot_lhs(cum_c, e_a, parts=2)
            beb_all = _sel_dot_lhs(jax.nn.sigmoid(gcol), e_b, parts=1)
            g_r = -a_row * _softplus(grow)
            cum_r = (_sel_dot_lhs(g_r, bd_upper), _sel_dot_lhs(g_r, bd_lower))
            for p in range(_GD_PAIRS):
                q_b = qkv[:, p * LANES:(p + 1) * LANES]
                k_b = qkv[:, GD_K_W + p * LANES:GD_K_W + (p + 1) * LANES]
                q_p = q_b.astype(F32)
                k_p = k_b.astype(F32)
                v_p = qkv[:, 2 * GD_K_W + p * LANES:2 * GD_K_W + (p + 1) * LANES].astype(F32)
                kexp = _expand_pair(k_b, lane_lo)
                kk = _dot_nt(k_b, kexp)
                qk = _dot_nt(q_b, kexp)
                for d in range(2):
                    pd = d * _GD_PAIRS + p
                    gcb = gcb_all[:, pd * LANES:(pd + 1) * LANES]
                    beb = beb_all[:, pd * LANES:(pd + 1) * LANES]
                    gcr = cum_r[d][_GD_ROW_A + pd:_GD_ROW_A + pd + 1, :]
                    decay = jnp.exp(jnp.where(incl[d], gcb - gcr, -jnp.inf))
                    lmat = jnp.where(strict[d], beb * kk * decay, 0.0)
                    eg = jnp.exp(gcb)
                    gl = gcb[L - 1:L, :] if d == 0 else gcb[0:1, :]
                    x = jnp.concatenate([v_p * beb, k_p * (beb * eg)], axis=1)
                    att_ref[pd, pl.ds(off, L), :] = (qk * decay).astype(BF)
                    wq_ref[pd, cidx, L:2 * L, :] = (q_p * eg).astype(BF)
                    kd_ref[pd, pl.ds(off, L), :] = (k_p * jnp.exp(gl - gcb)).astype(BF)
                    egl_ref[cidx, pd] = jnp.broadcast_to(jnp.exp(gl), (8, LANES))
                    chains.append(dict(pd=pd, cidx=cidx, off=off, p=lmat.astype(BF), l=lmat, x=x))

        for ch in chains:
            ch["t"] = eye_pair - ch["l"]
            ch["p"] = _dot(ch["p"], _expand_pair(ch["p"], lane_lo)).astype(BF)
        power = 2
        while power < L:
            last = 2 * power >= L
            for ch in chains:
                rhs = _expand_pair(ch["t"].astype(BF), lane_lo)
                if not last:
                    rhs = jnp.concatenate([_expand_pair(ch["p"], lane_lo), rhs], axis=1)
                ch["y"] = _dot(ch["p"], rhs)
            for ch in chains:
                if not last:
                    ch["p"] = ch["y"][:, :LANES].astype(BF)
                ch["t"] = ch["t"] + ch["y"][:, -LANES:]
            power *= 2
        for ch in chains:
            xb = ch["x"].astype(BF)
            rhs = jnp.concatenate([_expand_pair(xb[:, :LANES], lane_lo),
                                   _expand_pair(xb[:, LANES:], lane_lo)], axis=1)
            sol = _dot(ch["t"].astype(BF), rhs)
            u_ref[ch["pd"], pl.ds(ch["off"], L), :] = sol[:, :LANES]
            wq_ref[ch["pd"], ch["cidx"], 0:L, :] = sol[:, LANES:].astype(BF)
        return carry

    lax.fori_loop(0, nc // _GD_A_CHUNKS, phase_a, 0, unroll=2)

    rb = lax.broadcasted_iota(jnp.int32, (LANES, LANES), 0) // GD_K
    cb = lax.broadcasted_iota(jnp.int32, (LANES, LANES), 1) // GD_V
    blockdiag = rb == cb

    def phase_b(i, states):
        items = []
        for d in range(2):
            cidx = i if d == 0 else nc - 1 - i
            off = pl.multiple_of(cidx * L, L)
            for p in range(_GD_PAIRS):
                pd = d * _GD_PAIRS + p
                items.append((pd, p, cidx, off))
        s_b = [s.astype(BF) for s in states]
        res = [_dot(wq_ref[pd, cidx], s_b[pd]) for pd, _, cidx, off in items]
        v_new = [(u_ref[pd, pl.ds(off, L), :] - res[pd][:L]).astype(BF) for pd, _, _, off in items]
        new_states = []
        for pd, p, cidx, off in items:
            upd = _dot_tn(kd_ref[pd, pl.ds(off, L), :], v_new[pd])
            new_states.append(states[pd] * egl_ref[cidx, pd][0:1, :] + jnp.where(blockdiag, upd, 0.0))
        for pd, p, cidx, off in items:
            o = res[pd][L:] + _dot(att_ref[pd, pl.ds(off, L), :], _expand_pair(v_new[pd], lane_lo))
            acc_ref[pd // _GD_PAIRS, pl.ds(off, L), p * LANES:(p + 1) * LANES] = o
        return tuple(new_states)

    init = tuple(jnp.zeros((LANES, LANES), F32) for _ in range(_GD_PD))
    lax.fori_loop(0, nc, phase_b, init, unroll=2)

    hn = hn_ref[...]
    gsum_v = _group_ones(GD_V_W, GD_V)

    def epilogue(j, carry):
        off = pl.multiple_of(j * rows, rows)
        x = acc_ref[0, pl.ds(off, rows), :] + acc_ref[1, pl.ds(off, rows), :]
        z = z_ref[pl.ds(off, rows), :].astype(F32)
        ms = _sel_dot_lhs(x * x, gsum_v) * (1.0 / GD_V)
        y = x * lax.rsqrt(ms + EPS) * hn
        y_ref[pl.ds(off, rows), :] = (y * (z * jax.nn.sigmoid(z))).astype(y_ref.dtype)
        return carry

    lax.fori_loop(0, t // rows, epilogue, 0, unroll=2)


def _gdn(qkv, z, g_col, g_rowp, a_col, a_rowp, hn):
    b, t, _ = qkv.shape
    nc = t // GD_CHUNK
    seq = lambda i: (i, 0, 0)
    const = lambda i: (0, 0)
    return pl.pallas_call(
        _gdn_kernel,
        out_shape=jax.ShapeDtypeStruct((b, t, GD_V_W), BF),
        grid=(b,),
        in_specs=[
            pl.BlockSpec((None, t, _GD_QKV_W), seq),
            pl.BlockSpec((None, t, GD_V_W), seq),
            pl.BlockSpec((None, t, LANES), seq),
            pl.BlockSpec((None, nc, PAIR_ROWS, LANES), lambda i: (i, 0, 0, 0)),
            pl.BlockSpec((1, LANES), const),
            pl.BlockSpec((PAIR_ROWS, LANES), const),
            pl.BlockSpec((1, GD_V_W), const),
        ],
        out_specs=pl.BlockSpec((None, t, GD_V_W), seq),
        scratch_shapes=[
            pltpu.VMEM((2, t, GD_V_W), F32),
            pltpu.VMEM((_GD_PD, t, LANES), F32),
            pltpu.VMEM((_GD_PD, nc, 2 * GD_CHUNK, LANES), BF),
            pltpu.VMEM((_GD_PD, t, LANES), BF),
            pltpu.VMEM((_GD_PD, t, LANES), BF),
            pltpu.VMEM((nc, _GD_PD, 8, LANES), F32),
        ],
        compiler_params=pltpu.CompilerParams(
            dimension_semantics=("parallel",), vmem_limit_bytes=VMEM_LIMIT),
    )(qkv, z, g_col, g_rowp, a_col, a_rowp, hn)


def _outproj_kernel(x_ref, ya_ref, yb_ref, yc_ref, w_ref, o_ref):
    acc = x_ref[...]
    acc = acc + _dot(ya_ref[...], w_ref[0:GM_WIDTH, :])
    acc = acc + _dot(yb_ref[...], w_ref[GM_WIDTH:GM_WIDTH + ML_V_W, :])
    acc = acc + _dot(yc_ref[...], w_ref[GM_WIDTH + ML_V_W:, :])
    o_ref[...] = acc


def _outproj(x2, ya, yb, yc, w, layer, tm):
    n = x2.shape[0]
    row = lambda i: (i, 0)
    return pl.pallas_call(
        _outproj_kernel,
        out_shape=jax.ShapeDtypeStruct((n, D_MODEL), F32),
        grid=(n // tm,),
        in_specs=[
            pl.BlockSpec((tm, D_MODEL), row),
            pl.BlockSpec((tm, GM_WIDTH), row),
            pl.BlockSpec((tm, ML_V_W), row),
            pl.BlockSpec((tm, GD_V_W), row),
            _layer_spec((D_MODEL, D_MODEL), layer),
        ],
        out_specs=pl.BlockSpec((tm, D_MODEL), row),
        compiler_params=pltpu.CompilerParams(
            dimension_semantics=("parallel",), vmem_limit_bytes=VMEM_LIMIT),
    )(x2, ya, yb, yc, w)


_FFN_HALO = 16
_FFN_CHUNK = 256
_FFN_DOWN_GROUP = 4
_FFN_SLOTS = 3


def _ffn_kernel(xm_ref, xp_ref, xn_ref, g_ref, wup_ref, cw_ref, cb_ref, wdn_ref, gf_ref, o_ref,
                h_ref, act_ref, up_ref, *, tiles_per_seq, final_norm):
    i = pl.program_id(0)
    tm = xm_ref.shape[0]
    g = g_ref[...]
    pos = i % tiles_per_seq

    def norm(x):
        return x * lax.rsqrt(jnp.mean(x * x, axis=-1, keepdims=True) + EPS) * g

    hp = jnp.where(pos == 0, 0.0, norm(xp_ref[...]))
    hn = jnp.where(pos == tiles_per_seq - 1, 0.0, norm(xn_ref[...]))
    xm = xm_ref[...]
    h_ref[0:_FFN_HALO, :] = hp.astype(BF)
    h_ref[_FFN_HALO:_FFN_HALO + tm, :] = norm(xm).astype(BF)
    h_ref[_FFN_HALO + tm:, :] = hn.astype(BF)
    hh = h_ref[...]
    cw = cw_ref[...]
    cb = cb_ref[...]
    o_ref[...] = xm

    def up_project(j):
        c0 = j * _FFN_CHUNK
        slot = j % _FFN_SLOTS
        up_ref[slot, :, 0:_FFN_CHUNK] = _dot(hh, wup_ref[:, c0:c0 + _FFN_CHUNK])
        up_ref[slot, :, _FFN_CHUNK:] = _dot(hh, wup_ref[:, D_FF + c0:D_FF + c0 + _FFN_CHUNK])

    def conv(slot, lanes, c0):
        sl = slice(c0, c0 + _FFN_CHUNK)
        return (cw[0:1, sl] * up_ref[slot, _FFN_HALO - 1:_FFN_HALO - 1 + tm, lanes]
                + cw[1:2, sl] * up_ref[slot, _FFN_HALO:_FFN_HALO + tm, lanes]
                + cw[2:3, sl] * up_ref[slot, _FFN_HALO + 1:_FFN_HALO + 1 + tm, lanes] + cb[:, sl])

    n_chunks = D_FF // _FFN_CHUNK
    for j in range(_FFN_SLOTS - 1):
        up_project(j)
    for j in range(n_chunks):
        if j + _FFN_SLOTS - 1 < n_chunks:
            up_project(j + _FFN_SLOTS - 1)
        c0 = j * _FFN_CHUNK
        gt = conv(j % _FFN_SLOTS, slice(0, _FFN_CHUNK), c0)
        vl = conv(j % _FFN_SLOTS, slice(_FFN_CHUNK, 2 * _FFN_CHUNK), D_FF + c0)
        act_ref[:, c0:c0 + _FFN_CHUNK] = (gt * jax.nn.sigmoid(gt) * vl).astype(BF)
        if (j + 1) % _FFN_DOWN_GROUP == 0 or j + 1 == n_chunks:
            k0 = (j // _FFN_DOWN_GROUP) * _FFN_DOWN_GROUP * _FFN_CHUNK
            k1 = c0 + _FFN_CHUNK
            o_ref[...] += _dot(act_ref[:, k0:k1], wdn_ref[k0:k1, :])

    if final_norm:
        y = o_ref[...]
        o_ref[...] = y * lax.rsqrt(jnp.mean(y * y, axis=-1, keepdims=True) + EPS) * gf_ref[...]


def _ffn(x2, gain, w_up, conv_w, conv_b, w_down, gain_final, layer, seq_len, tm, final_norm):
    n = x2.shape[0]
    hb = tm // _FFN_HALO
    nhb = n // _FFN_HALO
    row = lambda i: (i, 0)
    const = lambda i: (0, 0)
    kern = functools.partial(_ffn_kernel, tiles_per_seq=seq_len // tm, final_norm=final_norm)
    return pl.pallas_call(
        kern,
        out_shape=jax.ShapeDtypeStruct((n, D_MODEL), F32),
        grid=(n // tm,),
        in_specs=[
            pl.BlockSpec((tm, D_MODEL), row),
            pl.BlockSpec((_FFN_HALO, D_MODEL), lambda i: (jnp.maximum(i * hb - 1, 0), 0)),
            pl.BlockSpec((_FFN_HALO, D_MODEL), lambda i: (jnp.minimum((i + 1) * hb, nhb - 1), 0)),
            pl.BlockSpec((1, D_MODEL), const),
            _layer_spec((D_MODEL, 2 * D_FF), layer, pipeline_mode=pl.Buffered(1)),
            pl.BlockSpec((FFN_CONV, 2 * D_FF), const),
            pl.BlockSpec((1, 2 * D_FF), const),
            _layer_spec((D_FF, D_MODEL), layer, pipeline_mode=pl.Buffered(1)),
            pl.BlockSpec((1, D_MODEL), const),
        ],
        out_specs=pl.BlockSpec((tm, D_MODEL), row),
        scratch_shapes=[
            pltpu.VMEM((tm + 2 * _FFN_HALO, D_MODEL), BF),
            pltpu.VMEM((tm, D_FF), BF),
            pltpu.VMEM((_FFN_SLOTS, tm + 2 * _FFN_HALO, 2 * _FFN_CHUNK), F32),
        ],
        compiler_params=pltpu.CompilerParams(
            dimension_semantics=("parallel",), vmem_limit_bytes=VMEM_LIMIT),
    )(x2, x2, x2, gain, w_up, conv_w, conv_b, w_down, gain_final)


_TM = 512
_TM_FFN = 1024


def _stacked_matmul_weights(w_in, w_out, w_up, w_down):
    depth = w_in.shape[0]
    g0 = 2 * GM_WIDTH + 2 * ML_QK_W + 2 * ML_V_W
    g1 = g0 + 4 * ML_HEADS
    c1 = g1 + _GD_QKV_W + GD_V_W
    q0 = 2 * GM_WIDTH
    col_scale = jnp.ones((_IN_MAIN,), F32).at[q0:q0 + ML_QK_W].set(ML_QK ** -0.5)
    w_main = (jnp.concatenate([w_in[:, :, :g0], w_in[:, :, g1:c1]], axis=2) * col_scale).astype(BF)
    w_gate = jnp.concatenate(
        [w_in[:, :, g0:g1], w_in[:, :, c1:], jnp.zeros((depth, D_MODEL, LANES - N_GATES), F32)],
        axis=2).astype(BF)
    return w_main, w_gate, w_out.astype(BF), w_up.astype(BF), w_down.astype(BF)


def _layer(x2, batch, seq_len, layer, big, norm_mix, gm_norm, gm_ws, gm_bs, ml_gate_bias, ml_head_norm,
           gd_conv, gd_a_log, gd_dt_bias, gd_head_norm, norm_ffn, ffn_conv, ffn_conv_b,
           norm_final, final_norm):
    n = x2.shape[0]
    nc = seq_len // GD_CHUNK
    w_main, w_gate, w_out, w_up, w_down = big
    gate_bias = jnp.concatenate(
        [ml_gate_bias, gd_dt_bias.reshape(-1), jnp.zeros((LANES - GATE_B,), F32)]).reshape(1, LANES)
    uv, qk, v, og, cq, z, gates = _inproj(
        x2, norm_mix.reshape(1, D_MODEL), w_main, w_gate, gate_bias, gd_conv, layer, seq_len, _TM)

    seq = lambda a: a.reshape(batch, seq_len, a.shape[-1])
    g_col = seq(gates)
    g32 = g_col[:, :, :N_GATES]
    g_rowp = g32.reshape(batch, nc, GD_CHUNK, 4, 2, _GD_PAIRS, 2).transpose(0, 1, 3, 4, 5, 6, 2)
    g_rowp = g_rowp.reshape(batch, nc, PAIR_ROWS, LANES)

    ws_cat = gm_ws.transpose(1, 0, 2).reshape(GM_CHUNK, GM_GROUPS * GM_CHUNK).astype(BF)
    bs_exp = jnp.repeat(gm_bs.T, GM_DIM, axis=1)
    ya = _gmlp(seq(uv), gm_norm.reshape(1, GM_WIDTH), ws_cat, bs_exp)

    yb = _mlstm(seq(qk), seq(v), seq(og), g_col, g_rowp, ml_head_norm.reshape(1, ML_V_W))

    a_flat = gd_a_log.reshape(-1)
    a_col = jnp.zeros((LANES,), F32).at[GATE_A:GATE_B].set(a_flat).reshape(1, LANES)
    a_rowp = jnp.zeros((PAIR_ROWS, LANES), F32).at[_GD_ROW_A:_GD_ROW_A + _GD_PD].set(
        jnp.repeat(a_flat.reshape(_GD_PD, 2), GD_K, axis=1))
    yc = _gdn(seq(cq), seq(z), g_col, g_rowp, a_col, a_rowp, gd_head_norm.reshape(1, GD_V_W))

    flat = lambda a: a.reshape(n, a.shape[-1])
    x2 = _outproj(x2, flat(ya), flat(yb), flat(yc), w_out, layer, _TM)
    return _ffn(x2, norm_ffn.reshape(1, D_MODEL), w_up, ffn_conv, ffn_conv_b.reshape(1, -1),
                w_down, norm_final.reshape(1, D_MODEL), layer, seq_len, _TM_FFN, final_norm)


def kernel(x, norm_mix, w_in, gm_norm, gm_ws, gm_bs, ml_gate_bias, ml_head_norm, gd_conv, gd_A_log,
           gd_dt_bias, gd_head_norm, w_out, norm_ffn, w_up, ffn_conv, ffn_conv_b, w_down, norm_final):
    batch, seq_len, _ = x.shape
    depth = w_in.shape[0]
    x2 = x.reshape(batch * seq_len, D_MODEL)
    big = _stacked_matmul_weights(w_in, w_out, w_up, w_down)
    for i in range(depth):
        x2 = _layer(x2, batch, seq_len, i, big, norm_mix[i], gm_norm[i], gm_ws[i], gm_bs[i],
                    ml_gate_bias[i], ml_head_norm[i], gd_conv[i], gd_A_log[i], gd_dt_bias[i],
                    gd_head_norm[i], norm_ffn[i], ffn_conv[i], ffn_conv_b[i],
                    norm_final, i == depth - 1)
    return x2.reshape(batch, seq_len, D_MODEL)
```

```python
import functools

import jax
import jax.numpy as jnp
from jax import lax
from jax.experimental import pallas as pl
from jax.experimental.pallas import tpu as pltpu

D_MODEL = 1024
GM_GROUPS = 4
GM_DIM = 64
GM_WIDTH = GM_GROUPS * GM_DIM
GM_CHUNK = 128
ML_HEADS = 4
ML_QK = 64
ML_V = 128
ML_QK_W = ML_HEADS * ML_QK
ML_V_W = ML_HEADS * ML_V
ML_CHUNK = 64
GD_HEADS = 4
GD_K = 64
GD_V = 64
GD_K_W = GD_HEADS * GD_K
GD_V_W = GD_HEADS * GD_V
GD_CHUNK = 64
GD_CONV = 5
D_FF = 2816
FFN_CONV = 3
EPS = 1e-6

LANES = 128
N_GATES = 32
GATE_I, GATE_F, GATE_A, GATE_B = 0, 8, 16, 24
PAIR_ROWS = 16
VMEM_LIMIT = 56 * 1024 * 1024

BF = jnp.bfloat16
F32 = jnp.float32


def _dot(a, b):
    return jnp.dot(a, b, preferred_element_type=F32)


def _dot_nt(a, b):
    return lax.dot_general(a, b, (((1,), (1,)), ((), ())), preferred_element_type=F32)


def _dot_tn(a, b):
    return lax.dot_general(a, b, (((0,), (0,)), ((), ())), preferred_element_type=F32)


def _split(x, parts):
    out = []
    for _ in range(parts - 1):
        hi = x.astype(BF)
        out.append(hi)
        x = x - hi.astype(F32)
    out.append(x.astype(BF))
    return out


def _sel_dot_rhs(sel, x, parts=3):
    acc = None
    for term in _split(x, parts):
        y = _dot(sel, term)
        acc = y if acc is None else acc + y
    return acc


def _sel_dot_lhs(x, sel, parts=3):
    acc = None
    for term in _split(x, parts):
        y = _dot(term, sel)
        acc = y if acc is None else acc + y
    return acc


def _expand_pair(xb, lane_lo):
    z = jnp.zeros((), xb.dtype)
    return jnp.concatenate([jnp.where(lane_lo, xb, z), jnp.where(lane_lo, z, xb)], axis=0)


def _run_interleaved(gens):
    results = [None] * len(gens)
    pending = list(range(len(gens)))
    while pending:
        for k in list(pending):
            try:
                next(gens[k])
            except StopIteration as stop:
                results[k] = stop.value
                pending.remove(k)
    return tuple(results)


def _running_max(x, reverse):
    n, w = x.shape
    sh = 1
    while sh < n:
        fill = jnp.full((sh, w), -jnp.inf, x.dtype)
        if reverse:
            shifted = jnp.concatenate([x[sh:], fill], axis=0)
        else:
            shifted = jnp.concatenate([fill, x[:n - sh]], axis=0)
        x = jnp.maximum(x, shifted)
        sh *= 2
    return x


def _pair_cumsum_mats():
    r = lax.broadcasted_iota(jnp.int32, (LANES, LANES), 0)
    c = lax.broadcasted_iota(jnp.int32, (LANES, LANES), 1)
    half = LANES // 2
    same = (r // half) == (c // half)
    prefix = jnp.where(same & (c >= r), 1.0, 0.0).astype(BF)
    suffix = jnp.where(same & (c <= r), 1.0, 0.0).astype(BF)
    return prefix, suffix


def _column_spread(first_col, n_cols, width):
    j = lax.broadcasted_iota(jnp.int32, (LANES, n_cols * width), 0)
    n = lax.broadcasted_iota(jnp.int32, (LANES, n_cols * width), 1) // width
    return jnp.where(j == first_col + n, 1.0, 0.0).astype(BF)


def _softplus(x):
    return jnp.maximum(x, 0.0) + jnp.log1p(jnp.exp(-jnp.abs(x)))


def _log_sigmoid(x):
    return jnp.minimum(x, 0.0) - jnp.log1p(jnp.exp(-jnp.abs(x)))


def _tri_masks(n):
    r = lax.broadcasted_iota(jnp.int32, (n, n), 0)
    c = lax.broadcasted_iota(jnp.int32, (n, n), 1)
    return c <= r, c >= r, c < r, c > r


def _group_ones(width, group):
    r = lax.broadcasted_iota(jnp.int32, (width, width), 0) // group
    c = lax.broadcasted_iota(jnp.int32, (width, width), 1) // group
    return jnp.where(r == c, 1.0, 0.0).astype(BF)


_IN_SEGS = ((0, 512), (512, 1024), (1024, 1536), (1536, 2048), (2048, 2816), (2816, 3072))
_IN_MAIN = 3072


_IN_HALO = 16
_IN_CONV_SEG = 4


def _inproj_kernel(x_ref, xp_ref, xn_ref, g_ref, w_ref, wg_ref, b_ref, cw_ref,
                   o_uv, o_qk, o_v, o_o, o_c, o_z, o_g, h_ref, c_ref, *, tiles_per_seq):
    tm = x_ref.shape[0]
    g = g_ref[...]
    pos = pl.program_id(0) % tiles_per_seq

    def norm(x):
        return x * lax.rsqrt(jnp.mean(x * x, axis=-1, keepdims=True) + EPS) * g

    h_ref[0:_IN_HALO, :] = jnp.where(pos == 0, 0.0, norm(xp_ref[...])).astype(BF)
    h_ref[_IN_HALO:_IN_HALO + tm, :] = norm(x_ref[...]).astype(BF)
    h_ref[_IN_HALO + tm:, :] = jnp.where(pos == tiles_per_seq - 1, 0.0, norm(xn_ref[...])).astype(BF)
    lo, hi = _IN_SEGS[_IN_CONV_SEG]
    c_ref[...] = _dot(h_ref[...], w_ref[:, lo:hi])
    h = h_ref[_IN_HALO:_IN_HALO + tm, :]
    for idx, o in enumerate((o_uv, o_qk, o_v, o_o, o_c, o_z)):
        if idx != _IN_CONV_SEG:
            lo, hi = _IN_SEGS[idx]
            o[...] = _dot(h, w_ref[:, lo:hi]).astype(o.dtype)
    o_g[...] = _dot(h, wg_ref[...]) + b_ref[...]

    cw = cw_ref[...]
    gsum = _group_ones(LANES, GD_K)
    pad = (GD_CONV - 1) // 2
    for s in range(_GD_QKV_W // LANES):
        sl = slice(s * LANES, (s + 1) * LANES)
        y = jnp.zeros((tm, LANES), F32)
        for tap in range(GD_CONV):
            r0 = _IN_HALO + tap - pad
            y = y + cw[tap:tap + 1, sl] * c_ref[r0:r0 + tm, sl]
        y = y * jax.nn.sigmoid(y)
        if s < 2 * GD_K_W // LANES:
            scale = GD_K ** -0.5 if s < GD_K_W // LANES else 1.0
            y = y * (lax.rsqrt(_sel_dot_lhs(y * y, gsum, parts=2) + EPS) * scale)
        o_c[:, sl] = y.astype(o_c.dtype)


def _layer_spec(shape, layer, **kw):
    zeros = (0,) * len(shape)
    return pl.BlockSpec((None,) + tuple(shape), lambda i: (layer,) + zeros, **kw)


def _inproj(x2, gain, w_main, w_gate, gate_bias, conv_w, layer, seq_len, tm):
    n = x2.shape[0]
    widths = [hi - lo for lo, hi in _IN_SEGS]
    out_shape = [jax.ShapeDtypeStruct((n, w), BF) for w in widths]
    out_shape.append(jax.ShapeDtypeStruct((n, LANES), F32))
    hb = tm // _IN_HALO
    nhb = n // _IN_HALO
    row = lambda i: (i, 0)
    const = lambda i: (0, 0)
    return pl.pallas_call(
        functools.partial(_inproj_kernel, tiles_per_seq=seq_len // tm),
        out_shape=out_shape,
        grid=(n // tm,),
        in_specs=[
            pl.BlockSpec((tm, D_MODEL), row),
            pl.BlockSpec((_IN_HALO, D_MODEL), lambda i: (jnp.maximum(i * hb - 1, 0), 0)),
            pl.BlockSpec((_IN_HALO, D_MODEL), lambda i: (jnp.minimum((i + 1) * hb, nhb - 1), 0)),
            pl.BlockSpec((1, D_MODEL), const),
            _layer_spec((D_MODEL, _IN_MAIN), layer),
            _layer_spec((D_MODEL, LANES), layer),
            pl.BlockSpec((1, LANES), const),
            pl.BlockSpec((GD_CONV, _GD_QKV_W), const),
        ],
        out_specs=[pl.BlockSpec((tm, w), row) for w in widths] + [pl.BlockSpec((tm, LANES), row)],
        scratch_shapes=[
            pltpu.VMEM((tm + 2 * _IN_HALO, D_MODEL), BF),
            pltpu.VMEM((tm + 2 * _IN_HALO, _GD_QKV_W), F32),
        ],
        compiler_params=pltpu.CompilerParams(
            dimension_semantics=("parallel",), vmem_limit_bytes=VMEM_LIMIT),
    )(x2, x2, x2, gain, w_main, w_gate, gate_bias, conv_w)


def _gmlp_kernel(uv_ref, gn_ref, ws_ref, bs_ref, y_ref):
    t = uv_ref.shape[0]
    gsum = _group_ones(GM_WIDTH, GM_DIM)
    rr = lax.broadcasted_iota(jnp.int32, (GM_GROUPS * GM_CHUNK, GM_WIDTH), 0) // GM_CHUNK
    cc = lax.broadcasted_iota(jnp.int32, (GM_GROUPS * GM_CHUNK, GM_WIDTH), 1) // GM_DIM
    blockdiag = rr == cc
    ws = ws_ref[...]
    bs = bs_ref[...]
    gn = gn_ref[...]

    def body(c, carry):
        off = pl.multiple_of(c * GM_CHUNK, GM_CHUNK)
        uv = uv_ref[pl.ds(off, GM_CHUNK), :].astype(F32)
        u = jax.nn.gelu(uv[:, :GM_WIDTH])
        v = jax.nn.gelu(uv[:, GM_WIDTH:])
        ms = _sel_dot_lhs(v * v, gsum, parts=2) * (1.0 / GM_DIM)
        vn = (v * lax.rsqrt(ms + EPS) * gn).astype(BF)
        vexp = jnp.where(blockdiag, jnp.concatenate([vn] * GM_GROUPS, axis=0), jnp.zeros((), BF))
        sg = _dot(ws, vexp) + bs
        y_ref[pl.ds(off, GM_CHUNK), :] = (u * sg).astype(y_ref.dtype)
        return carry

    lax.fori_loop(0, t // GM_CHUNK, body, 0, unroll=4)


def _gmlp(uv, gn, ws_cat, bs_exp):
    b, t, _ = uv.shape
    const = lambda i: (0, 0)
    return pl.pallas_call(
        _gmlp_kernel,
        out_shape=jax.ShapeDtypeStruct((b, t, GM_WIDTH), BF),
        grid=(b,),
        in_specs=[
            pl.BlockSpec((None, t, 2 * GM_WIDTH), lambda i: (i, 0, 0)),
            pl.BlockSpec((1, GM_WIDTH), const),
            pl.BlockSpec((GM_CHUNK, GM_GROUPS * GM_CHUNK), const),
            pl.BlockSpec((GM_CHUNK, GM_WIDTH), const),
        ],
        out_specs=pl.BlockSpec((None, t, GM_WIDTH), lambda i: (i, 0, 0)),
        compiler_params=pltpu.CompilerParams(
            dimension_semantics=("parallel",), vmem_limit_bytes=VMEM_LIMIT),
    )(uv, gn, ws_cat, bs_exp)


_ML_GROUP = 4


def _mlstm_kernel(qk_ref, v_ref, og_ref, gc_ref, gr_ref, hn_ref, y_ref,
                  acc_ref, bc_ref, cm_ref, kw_ref, st_ref):
    t = qk_ref.shape[0]
    L = ML_CHUNK
    nc = t // L
    lower, upper, _, _ = _tri_masks(L)
    lower_b = jnp.where(lower, 1.0, 0.0).astype(BF)
    upper_b = jnp.where(upper, 1.0, 0.0).astype(BF)
    ones_v = jnp.ones((L, ML_V), BF)
    zero_v = jnp.zeros((L, ML_V), BF)
    zero_c = jnp.zeros((ML_QK, ML_V), BF)
    tt = lax.broadcasted_iota(jnp.int32, (L, LANES), 0)
    ln = lax.broadcasted_iota(jnp.int32, (L, LANES), 1)
    lane_lo = ln < ML_QK
    mask = (ln % ML_QK <= tt, ln % ML_QK >= tt)
    f_cols = (ln >= GATE_F) & (ln < GATE_F + 2 * ML_HEADS)
    pair_prefix, pair_suffix = _pair_cumsum_mats()
    ro = lax.broadcasted_iota(jnp.int32, (LANES, 2 * ML_V), 0) // ML_QK
    co = lax.broadcasted_iota(jnp.int32, (LANES, 2 * ML_V), 1) // ML_V
    pair_rowsum = jnp.where(ro == co, 1.0, 0.0).astype(BF)
    bw_cols = (ln // ML_HEADS) % 2 == 1
    spread_k_all = _column_spread(GATE_F, 2 * ML_HEADS, ML_QK)

    def gate_stats(c):
        off = pl.multiple_of(c * L, L)
        gcol = gc_ref[pl.ds(off, L), :]
        grow = gr_ref[c]
        lf = _log_sigmoid(gcol)
        lf_r = _log_sigmoid(grow)
        bc = jnp.where(bw_cols, _sel_dot_rhs(upper_b, lf), _sel_dot_rhs(lower_b, lf))
        cum_f = _sel_dot_lhs(lf_r, pair_prefix)
        cum_b = _sel_dot_lhs(lf_r, pair_suffix)
        yield
        ig = pltpu.roll(gcol, GATE_F - GATE_I, 1)
        b_last = jnp.where(bw_cols[0:1], bc[0:1, :], bc[L - 1:L, :])
        a = b_last - bc + ig
        m_loc = jnp.max(a, axis=0, keepdims=True)
        wa_k = _sel_dot_lhs(jnp.where(f_cols, jnp.exp(a - m_loc), 0.0), spread_k_all, parts=1)
        yield
        r = ig - bc
        bc_ref[pl.ds(off, L), :] = bc
        cm_ref[pl.ds(off, L), :] = jnp.where(bw_cols, _running_max(r, True), _running_max(r, False))
        hp = ML_HEADS // 2
        rr = jnp.concatenate([grow[0:hp] - cum_f[ML_HEADS:ML_HEADS + hp],
                              grow[hp:2 * hp] - cum_b[ML_HEADS + hp:ML_HEADS + 2 * hp]], axis=0)
        st_ref[c] = jnp.concatenate([b_last, m_loc, rr, jnp.zeros((8 - 2 - 2 * hp, LANES), F32)], axis=0)
        yield
        for p in range(hp):
            k_p = qk_ref[pl.ds(off, L), ML_QK_W + p * LANES:ML_QK_W + (p + 1) * LANES].astype(F32)
            for d in range(2):
                lanes = slice((2 * d + p) * LANES, (2 * d + p + 1) * LANES)
                kw_ref[pl.ds(off, L), lanes] = (k_p * wa_k[:, lanes]).astype(BF)

    def gate_stats_group(j, carry):
        _run_interleaved([gate_stats(j * _ML_GROUP + k) for k in range(_ML_GROUP)])
        return carry

    lax.fori_loop(0, nc // _ML_GROUP, gate_stats_group, 0)

    def chunk_pre(cidx, d, m):
        off = pl.multiple_of(cidx * L, L)
        qkc = qk_ref[pl.ds(off, L), :]
        vc = v_ref[pl.ds(off, L), :]
        col0 = GATE_F + ML_HEADS * d
        spread_k = _column_spread(col0, ML_HEADS, ML_QK)
        spread_v = _column_spread(col0, ML_HEADS, ML_V)
        bc = bc_ref[pl.ds(off, L), :]
        stats = st_ref[cidx]
        b_last = stats[0:1, :]
        m_loc = stats[1:2, :]
        mx = jnp.maximum(m, cm_ref[pl.ds(off, L), :])
        m_new = jnp.maximum(b_last + m, m_loc)
        f_rows = jnp.concatenate(
            [jnp.exp(b_last + m - m_new), jnp.exp(m_loc - m_new), jnp.zeros((6, LANES), F32)], axis=0)
        clean = lambda x: jnp.where(f_cols[:x.shape[0]], x, 0.0)
        mx_k = _sel_dot_lhs(clean(mx), spread_k, parts=2)
        f_v = _sel_dot_lhs(clean(f_rows), spread_v)
        w_col = jnp.exp(m - mx)
        e_col = jnp.exp(-(bc + mx))

        def head_lanes(x, p):
            cols = [jnp.broadcast_to(x[:, col0 + 2 * p + hh:col0 + 2 * p + hh + 1], (L, ML_V)) for hh in range(2)]
            return jnp.concatenate(cols, axis=1)

        yield
        pairs = []
        for p in range(ML_HEADS // 2):
            row = 2 * d + p
            q_p = qkc[:, p * LANES:(p + 1) * LANES]
            k_p = qkc[:, ML_QK_W + p * LANES:ML_QK_W + (p + 1) * LANES]
            v0 = vc[:, 2 * p * ML_V:(2 * p + 1) * ML_V]
            v1 = vc[:, (2 * p + 1) * ML_V:(2 * p + 2) * ML_V]
            qk = _dot_nt(q_p, _expand_pair(k_p, lane_lo))
            rr = stats[2 + row:3 + row, :]
            s = qk * jnp.exp(jnp.where(mask[d], rr - mx_k[:, p * LANES:(p + 1) * LANES], -jnp.inf))
            rhs_s = jnp.concatenate(
                [jnp.concatenate([v0, zero_v], axis=1), jnp.concatenate([zero_v, v1], axis=1)], axis=0)
            intra = _dot(s.astype(BF), jnp.concatenate([rhs_s, pair_rowsum], axis=1))
            yield
            kw = kw_ref[pl.ds(off, L), row * LANES:(row + 1) * LANES]
            upd = _dot_tn(kw, jnp.concatenate([v0, v1, ones_v], axis=1))
            pairs.append(dict(q=q_p, intra=intra, upd=upd,
                              w2=head_lanes(w_col, p), e2=head_lanes(e_col, p)))
            yield
        return dict(d=d, off=off, pairs=pairs, f_v=f_v)

    def chunk_post(pre, heads):
        new_heads = []
        for p, pp in enumerate(pre["pairs"]):
            (c0, n0), (c1, n1) = heads[2 * p], heads[2 * p + 1]
            rhs_c = jnp.concatenate(
                [jnp.concatenate([c0.astype(BF), zero_c, n0.astype(BF), zero_c], axis=1),
                 jnp.concatenate([zero_c, c1.astype(BF), zero_c, n1.astype(BF)], axis=1)], axis=0)
            inter = _dot(pp["q"], rhs_c)
            na = jnp.concatenate([pp["w2"], pp["w2"]], axis=1) * inter + pp["intra"]
            out = na[:, :2 * ML_V] / jnp.maximum(jnp.abs(na[:, 2 * ML_V:]), pp["e2"])
            acc_ref[pre["d"], pl.ds(pre["off"], L), 2 * p * ML_V:(2 * p + 2) * ML_V] = out
            for hh, (c, n) in enumerate(((c0, n0), (c1, n1))):
                h = 2 * p + hh
                f_old = pre["f_v"][0:1, h * ML_V:(h + 1) * ML_V]
                f_loc = pre["f_v"][1:2, h * ML_V:(h + 1) * ML_V]
                rs = slice(hh * ML_QK, (hh + 1) * ML_QK)
                new_heads.append((f_old * c + f_loc * pp["upd"][rs, hh * ML_V:(hh + 1) * ML_V],
                                  f_old * n + f_loc * pp["upd"][rs, 2 * ML_V:3 * ML_V]))
        return tuple(new_heads)

    def body(j, carry):
        (heads_f, m_f), (heads_b, m_b) = carry
        chunks = [(j * _ML_GROUP + k, nc - 1 - (j * _ML_GROUP + k)) for k in range(_ML_GROUP)]
        ms_f, ms_b = [m_f], [m_b]
        for cf, cb in chunks:
            ms_f.append(jnp.maximum(st_ref[cf][0:1, :] + ms_f[-1], st_ref[cf][1:2, :]))
            ms_b.append(jnp.maximum(st_ref[cb][0:1, :] + ms_b[-1], st_ref[cb][1:2, :]))
        pres = _run_interleaved(
            [chunk_pre(cf, 0, ms_f[k]) for k, (cf, _) in enumerate(chunks)]
            + [chunk_pre(cb, 1, ms_b[k]) for k, (_, cb) in enumerate(chunks)])
        for k in range(_ML_GROUP):
            heads_f = chunk_post(pres[k], heads_f)
            heads_b = chunk_post(pres[_ML_GROUP + k], heads_b)
        return (heads_f, ms_f[-1]), (heads_b, ms_b[-1])

    zero_state = jnp.zeros((ML_QK, ML_V), F32)
    init = (tuple((zero_state, zero_state) for _ in range(ML_HEADS)), jnp.zeros((1, LANES), F32))
    lax.fori_loop(0, nc // _ML_GROUP, body, (init, init))

    hn = hn_ref[...]
    rows = 256

    def epilogue(j, carry):
        off = pl.multiple_of(j * rows, rows)
        hb = acc_ref[0, pl.ds(off, rows), :] + acc_ref[1, pl.ds(off, rows), :]
        gate = jax.nn.sigmoid(og_ref[pl.ds(off, rows), :].astype(F32))
        for h in range(ML_HEADS):
            sl = slice(h * ML_V, (h + 1) * ML_V)
            x = hb[:, sl]
            y = x * lax.rsqrt(jnp.mean(x * x, axis=-1, keepdims=True) + EPS) * hn[:, sl]
            y_ref[pl.ds(off, rows), sl] = (y * gate[:, sl]).astype(y_ref.dtype)
        return carry

    lax.fori_loop(0, t // rows, epilogue, 0, unroll=2)


def _mlstm(qk, v, og, g_col, g_row, hn):
    b, t, _ = qk.shape
    nc = t // ML_CHUNK
    seq = lambda i: (i, 0, 0)
    return pl.pallas_call(
        _mlstm_kernel,
        out_shape=jax.ShapeDtypeStruct((b, t, ML_V_W), BF),
        grid=(b,),
        in_specs=[
            pl.BlockSpec((None, t, 2 * ML_QK_W), seq),
            pl.BlockSpec((None, t, ML_V_W), seq),
            pl.BlockSpec((None, t, ML_V_W), seq),
            pl.BlockSpec((None, t, LANES), seq),
            pl.BlockSpec((None, nc, PAIR_ROWS, LANES), lambda i: (i, 0, 0, 0)),
            pl.BlockSpec((1, ML_V_W), lambda i: (0, 0)),
        ],
        out_specs=pl.BlockSpec((None, t, ML_V_W), seq),
        scratch_shapes=[
            pltpu.VMEM((2, t, ML_V_W), F32),
            pltpu.VMEM((t, LANES), F32),
            pltpu.VMEM((t, LANES), F32),
            pltpu.VMEM((t, 2 * ML_QK_W), BF),
            pltpu.VMEM((nc, 8, LANES), F32),
        ],
        compiler_params=pltpu.CompilerParams(
            dimension_semantics=("parallel",), vmem_limit_bytes=VMEM_LIMIT),
    )(qk, v, og, g_col, g_row, hn)


_GD_QKV_W = 2 * GD_K_W + GD_V_W
_GD_PAIRS = GD_HEADS // 2
_GD_PD = 2 * _GD_PAIRS
_GD_ROW_A = 8
_GD_A_CHUNKS = 4


def _gdn_kernel(qkv_ref, z_ref, gc_ref, gr_ref, acol_ref, arow_ref, hn_ref, y_ref,
                acc_ref, u_ref, wq_ref, att_ref, kd_ref, egl_ref):
    t = qkv_ref.shape[0]
    L = GD_CHUNK
    nc = t // L
    rows = 256
    incl_lo, incl_up, _, _ = _tri_masks(L)
    lower_b = jnp.where(incl_lo, 1.0, 0.0).astype(BF)
    upper_b = jnp.where(incl_up, 1.0, 0.0).astype(BF)

    tt = lax.broadcasted_iota(jnp.int32, (L, LANES), 0)
    ln = lax.broadcasted_iota(jnp.int32, (L, LANES), 1)
    ss = ln % GD_K
    lane_lo = ln < GD_K
    incl = (ss <= tt, ss >= tt)
    strict = (ss < tt, ss > tt)
    eye_pair = jnp.where(ss == tt, 1.0, 0.0)
    bw_cols = (ln // GD_HEADS) % 2 == 1
    r2 = lax.broadcasted_iota(jnp.int32, (2 * L, LANES), 0)
    c2 = lax.broadcasted_iota(jnp.int32, (2 * L, LANES), 1)
    same_half = (r2 // L) == (c2 // GD_K)
    bd_upper = jnp.where(same_half & (c2 % GD_K >= r2 % L), 1.0, 0.0).astype(BF)
    bd_lower = jnp.where(same_half & (c2 % GD_K <= r2 % L), 1.0, 0.0).astype(BF)
    ej = lax.broadcasted_iota(jnp.int32, (LANES, _GD_PD * LANES), 0)
    en = lax.broadcasted_iota(jnp.int32, (LANES, _GD_PD * LANES), 1) // GD_K
    e_a = jnp.where(ej == GATE_A + en, 1.0, 0.0).astype(BF)
    e_b = jnp.where(ej == GATE_B + en, 1.0, 0.0).astype(BF)
    a_col = jnp.exp(acol_ref[...])
    a_row = jnp.exp(arow_ref[...])

    def phase_a(j, carry):
        chains = []
        for ci in range(_GD_A_CHUNKS):
            cidx = j * _GD_A_CHUNKS + ci
            off = pl.multiple_of(cidx * L, L)
            gcol = gc_ref[pl.ds(off, L), :]
            grow = gr_ref[cidx]
            qkv = qkv_ref[pl.ds(off, L), :]
            g_c = -a_col * _softplus(gcol)
            cum_c = jnp.where(bw_cols, _sel_dot_rhs(upper_b, g_c), _sel_dot_rhs(lower_b, g_c))
            gcb_all = _sel_dot_lhs(cum_c, e_a, parts=2)
            beb_all = _sel_dot_lhs(jax.nn.sigmoid(gcol), e_b, parts=1)
            g_r = -a_row * _softplus(grow)
            cum_r = (_sel_dot_lhs(g_r, bd_upper), _sel_dot_lhs(g_r, bd_lower))
            for p in range(_GD_PAIRS):
                q_b = qkv[:, p * LANES:(p + 1) * LANES]
                k_b = qkv[:, GD_K_W + p * LANES:GD_K_W + (p + 1) * LANES]
                q_p = q_b.astype(F32)
                k_p = k_b.astype(F32)
                v_p = qkv[:, 2 * GD_K_W + p * LANES:2 * GD_K_W + (p + 1) * LANES].astype(F32)
                kexp = _expand_pair(k_b, lane_lo)
                kk = _dot_nt(k_b, kexp)
                qk = _dot_nt(q_b, kexp)
                for d in range(2):
                    pd = d * _GD_PAIRS + p
                    gcb = gcb_all[:, pd * LANES:(pd + 1) * LANES]
                    beb = beb_all[:, pd * LANES:(pd + 1) * LANES]
                    gcr = cum_r[d][_GD_ROW_A + pd:_GD_ROW_A + pd + 1, :]
                    decay = jnp.exp(jnp.where(incl[d], gcb - gcr, -jnp.inf))
                    lmat = jnp.where(strict[d], beb * kk * decay, 0.0)
                    eg = jnp.exp(gcb)
                    gl = gcb[L - 1:L, :] if d == 0 else gcb[0:1, :]
                    x = jnp.concatenate([v_p * beb, k_p * (beb * eg)], axis=1)
                    att_ref[pd, pl.ds(off, L), :] = (qk * decay).astype(BF)
                    wq_ref[pd, cidx, L:2 * L, :] = (q_p * eg).astype(BF)
                    kd_ref[pd, pl.ds(off, L), :] = (k_p * jnp.exp(gl - gcb)).astype(BF)
                    egl_ref[cidx, pd] = jnp.broadcast_to(jnp.exp(gl), (8, LANES))
                    chains.append(dict(pd=pd, cidx=cidx, off=off, p=lmat.astype(BF), l=lmat, x=x))

        for ch in chains:
            ch["t"] = eye_pair - ch["l"]
            ch["p"] = _dot(ch["p"], _expand_pair(ch["p"], lane_lo)).astype(BF)
        power = 2
        while power < L:
            last = 2 * power >= L
            for ch in chains:
                rhs = _expand_pair(ch["t"].astype(BF), lane_lo)
                if not last:
                    rhs = jnp.concatenate([_expand_pair(ch["p"], lane_lo), rhs], axis=1)
                ch["y"] = _dot(ch["p"], rhs)
            for ch in chains:
                if not last:
                    ch["p"] = ch["y"][:, :LANES].astype(BF)
                ch["t"] = ch["t"] + ch["y"][:, -LANES:]
            power *= 2
        for ch in chains:
            xb = ch["x"].astype(BF)
            rhs = jnp.concatenate([_expand_pair(xb[:, :LANES], lane_lo),
                                   _expand_pair(xb[:, LANES:], lane_lo)], axis=1)
            sol = _dot(ch["t"].astype(BF), rhs)
            u_ref[ch["pd"], pl.ds(ch["off"], L), :] = sol[:, :LANES]
            wq_ref[ch["pd"], ch["cidx"], 0:L, :] = sol[:, LANES:].astype(BF)
        return carry

    lax.fori_loop(0, nc // _GD_A_CHUNKS, phase_a, 0)

    rb = lax.broadcasted_iota(jnp.int32, (LANES, LANES), 0) // GD_K
    cb = lax.broadcasted_iota(jnp.int32, (LANES, LANES), 1) // GD_V
    blockdiag = rb == cb

    def phase_b(i, states):
        items = []
        for d in range(2):
            cidx = i if d == 0 else nc - 1 - i
            off = pl.multiple_of(cidx * L, L)
            for p in range(_GD_PAIRS):
                pd = d * _GD_PAIRS + p
                items.append((pd, p, cidx, off))
        s_b = [s.astype(BF) for s in states]
        res = [_dot(wq_ref[pd, cidx], s_b[pd]) for pd, _, cidx, off in items]
        v_new = [(u_ref[pd, pl.ds(off, L), :] - res[pd][:L]).astype(BF) for pd, _, _, off in items]
        new_states = []
        for pd, p, cidx, off in items:
            upd = _dot_tn(kd_ref[pd, pl.ds(off, L), :], v_new[pd])
            new_states.append(states[pd] * egl_ref[cidx, pd][0:1, :] + jnp.where(blockdiag, upd, 0.0))
        for pd, p, cidx, off in items:
            o = res[pd][L:] + _dot(att_ref[pd, pl.ds(off, L), :], _expand_pair(v_new[pd], lane_lo))
            acc_ref[pd // _GD_PAIRS, pl.ds(off, L), p * LANES:(p + 1) * LANES] = o
        return tuple(new_states)

    init = tuple(jnp.zeros((LANES, LANES), F32) for _ in range(_GD_PD))
    lax.fori_loop(0, nc, phase_b, init, unroll=2)

    hn = hn_ref[...]
    gsum_v = _group_ones(GD_V_W, GD_V)

    def epilogue(j, carry):
        off = pl.multiple_of(j * rows, rows)
        x = acc_ref[0, pl.ds(off, rows), :] + acc_ref[1, pl.ds(off, rows), :]
        z = z_ref[pl.ds(off, rows), :].astype(F32)
        ms = _sel_dot_lhs(x * x, gsum_v) * (1.0 / GD_V)
        y = x * lax.rsqrt(ms + EPS) * hn
        y_ref[pl.ds(off, rows), :] = (y * (z * jax.nn.sigmoid(z))).astype(y_ref.dtype)
        return carry

    lax.fori_loop(0, t // rows, epilogue, 0, unroll=2)


def _gdn(qkv, z, g_col, g_rowp, a_col, a_rowp, hn):
    b, t, _ = qkv.shape
    nc = t // GD_CHUNK
    seq = lambda i: (i, 0, 0)
    const = lambda i: (0, 0)
    return pl.pallas_call(
        _gdn_kernel,
        out_shape=jax.ShapeDtypeStruct((b, t, GD_V_W), BF),
        grid=(b,),
        in_specs=[
            pl.BlockSpec((None, t, _GD_QKV_W), seq),
            pl.BlockSpec((None, t, GD_V_W), seq),
            pl.BlockSpec((None, t, LANES), seq),
            pl.BlockSpec((None, nc, PAIR_ROWS, LANES), lambda i: (i, 0, 0, 0)),
            pl.BlockSpec((1, LANES), const),
            pl.BlockSpec((PAIR_ROWS, LANES), const),
            pl.BlockSpec((1, GD_V_W), const),
        ],
        out_specs=pl.BlockSpec((None, t, GD_V_W), seq),
        scratch_shapes=[
            pltpu.VMEM((2, t, GD_V_W), F32),
            pltpu.VMEM((_GD_PD, t, LANES), F32),
            pltpu.VMEM((_GD_PD, nc, 2 * GD_CHUNK, LANES), BF),
            pltpu.VMEM((_GD_PD, t, LANES), BF),
            pltpu.VMEM((_GD_PD, t, LANES), BF),
            pltpu.VMEM((nc, _GD_PD, 8, LANES), F32),
        ],
        compiler_params=pltpu.CompilerParams(
            dimension_semantics=("parallel",), vmem_limit_bytes=VMEM_LIMIT),
    )(qkv, z, g_col, g_rowp, a_col, a_rowp, hn)


def _outproj_kernel(x_ref, ya_ref, yb_ref, yc_ref, w_ref, o_ref):
    acc = x_ref[...]
    acc = acc + _dot(ya_ref[...], w_ref[0:GM_WIDTH, :])
    acc = acc + _dot(yb_ref[...], w_ref[GM_WIDTH:GM_WIDTH + ML_V_W, :])
    acc = acc + _dot(yc_ref[...], w_ref[GM_WIDTH + ML_V_W:, :])
    o_ref[...] = acc


def _outproj(x2, ya, yb, yc, w, layer, tm):
    n = x2.shape[0]
    row = lambda i: (i, 0)
    return pl.pallas_call(
        _outproj_kernel,
        out_shape=jax.ShapeDtypeStruct((n, D_MODEL), F32),
        grid=(n // tm,),
        in_specs=[
            pl.BlockSpec((tm, D_MODEL), row),
            pl.BlockSpec((tm, GM_WIDTH), row),
            pl.BlockSpec((tm, ML_V_W), row),
            pl.BlockSpec((tm, GD_V_W), row),
            _layer_spec((D_MODEL, D_MODEL), layer),
        ],
        out_specs=pl.BlockSpec((tm, D_MODEL), row),
        compiler_params=pltpu.CompilerParams(
            dimension_semantics=("parallel",), vmem_limit_bytes=VMEM_LIMIT),
    )(x2, ya, yb, yc, w)


_FFN_HALO = 16
_FFN_CHUNK = 256
_FFN_DOWN_GROUP = 4
_FFN_SLOTS = 3


def _ffn_kernel(xm_ref, xp_ref, xn_ref, g_ref, wup_ref, cw_ref, cb_ref, wdn_ref, gf_ref, o_ref,
                h_ref, act_ref, up_ref, *, tiles_per_seq, final_norm):
    i = pl.program_id(0)
    tm = xm_ref.shape[0]
    g = g_ref[...]
    pos = i % tiles_per_seq

    def norm(x):
        return x * lax.rsqrt(jnp.mean(x * x, axis=-1, keepdims=True) + EPS) * g

    hp = jnp.where(pos == 0, 0.0, norm(xp_ref[...]))
    hn = jnp.where(pos == tiles_per_seq - 1, 0.0, norm(xn_ref[...]))
    xm = xm_ref[...]
    h_ref[0:_FFN_HALO, :] = hp.astype(BF)
    h_ref[_FFN_HALO:_FFN_HALO + tm, :] = norm(xm).astype(BF)
    h_ref[_FFN_HALO + tm:, :] = hn.astype(BF)
    hh = h_ref[...]
    cw = cw_ref[...]
    cb = cb_ref[...]
    o_ref[...] = xm

    def up_project(j):
        c0 = j * _FFN_CHUNK
        slot = j % _FFN_SLOTS
        up_ref[slot, :, 0:_FFN_CHUNK] = _dot(hh, wup_ref[:, c0:c0 + _FFN_CHUNK])
        up_ref[slot, :, _FFN_CHUNK:] = _dot(hh, wup_ref[:, D_FF + c0:D_FF + c0 + _FFN_CHUNK])

    def conv(slot, lanes, c0):
        sl = slice(c0, c0 + _FFN_CHUNK)
        return (cw[0:1, sl] * up_ref[slot, _FFN_HALO - 1:_FFN_HALO - 1 + tm, lanes]
                + cw[1:2, sl] * up_ref[slot, _FFN_HALO:_FFN_HALO + tm, lanes]
                + cw[2:3, sl] * up_ref[slot, _FFN_HALO + 1:_FFN_HALO + 1 + tm, lanes] + cb[:, sl])

    n_chunks = D_FF // _FFN_CHUNK
    for j in range(_FFN_SLOTS - 1):
        up_project(j)
    for j in range(n_chunks):
        if j + _FFN_SLOTS - 1 < n_chunks:
            up_project(j + _FFN_SLOTS - 1)
        c0 = j * _FFN_CHUNK
        gt = conv(j % _FFN_SLOTS, slice(0, _FFN_CHUNK), c0)
        vl = conv(j % _FFN_SLOTS, slice(_FFN_CHUNK, 2 * _FFN_CHUNK), D_FF + c0)
        act_ref[:, c0:c0 + _FFN_CHUNK] = (gt * jax.nn.sigmoid(gt) * vl).astype(BF)
        if (j + 1) % _FFN_DOWN_GROUP == 0 or j + 1 == n_chunks:
            k0 = (j // _FFN_DOWN_GROUP) * _FFN_DOWN_GROUP * _FFN_CHUNK
            k1 = c0 + _FFN_CHUNK
            o_ref[...] += _dot(act_ref[:, k0:k1], wdn_ref[k0:k1, :])

    if final_norm:
        y = o_ref[...]
        o_ref[...] = y * lax.rsqrt(jnp.mean(y * y, axis=-1, keepdims=True) + EPS) * gf_ref[...]


def _ffn(x2, gain, w_up, conv_w, conv_b, w_down, gain_final, layer, seq_len, tm, final_norm):
    n = x2.shape[0]
    hb = tm // _FFN_HALO
    nhb = n // _FFN_HALO
    row = lambda i: (i, 0)
    const = lambda i: (0, 0)
    kern = functools.partial(_ffn_kernel, tiles_per_seq=seq_len // tm, final_norm=final_norm)
    return pl.pallas_call(
        kern,
        out_shape=jax.ShapeDtypeStruct((n, D_MODEL), F32),
        grid=(n // tm,),
        in_specs=[
            pl.BlockSpec((tm, D_MODEL), row),
            pl.BlockSpec((_FFN_HALO, D_MODEL), lambda i: (jnp.maximum(i * hb - 1, 0), 0)),
            pl.BlockSpec((_FFN_HALO, D_MODEL), lambda i: (jnp.minimum((i + 1) * hb, nhb - 1), 0)),
            pl.BlockSpec((1, D_MODEL), const),
            _layer_spec((D_MODEL, 2 * D_FF), layer, pipeline_mode=pl.Buffered(1)),
            pl.BlockSpec((FFN_CONV, 2 * D_FF), const),
            pl.BlockSpec((1, 2 * D_FF), const),
            _layer_spec((D_FF, D_MODEL), layer, pipeline_mode=pl.Buffered(1)),
            pl.BlockSpec((1, D_MODEL), const),
        ],
        out_specs=pl.BlockSpec((tm, D_MODEL), row),
        scratch_shapes=[
            pltpu.VMEM((tm + 2 * _FFN_HALO, D_MODEL), BF),
            pltpu.VMEM((tm, D_FF), BF),
            pltpu.VMEM((_FFN_SLOTS, tm + 2 * _FFN_HALO, 2 * _FFN_CHUNK), F32),
        ],
        compiler_params=pltpu.CompilerParams(
            dimension_semantics=("parallel",), vmem_limit_bytes=VMEM_LIMIT),
    )(x2, x2, x2, gain, w_up, conv_w, conv_b, w_down, gain_final)


_TM = 512
_TM_FFN = 1024


def _stacked_matmul_weights(w_in, w_out, w_up, w_down):
    depth = w_in.shape[0]
    g0 = 2 * GM_WIDTH + 2 * ML_QK_W + 2 * ML_V_W
    g1 = g0 + 4 * ML_HEADS
    c1 = g1 + _GD_QKV_W + GD_V_W
    q0 = 2 * GM_WIDTH
    col_scale = jnp.ones((_IN_MAIN,), F32).at[q0:q0 + ML_QK_W].set(ML_QK ** -0.5)
    w_main = (jnp.concatenate([w_in[:, :, :g0], w_in[:, :, g1:c1]], axis=2) * col_scale).astype(BF)
    w_gate = jnp.concatenate(
        [w_in[:, :, g0:g1], w_in[:, :, c1:], jnp.zeros((depth, D_MODEL, LANES - N_GATES), F32)],
        axis=2).astype(BF)
    return w_main, w_gate, w_out.astype(BF), w_up.astype(BF), w_down.astype(BF)


def _layer(x2, batch, seq_len, layer, big, norm_mix, gm_norm, gm_ws, gm_bs, ml_gate_bias, ml_head_norm,
           gd_conv, gd_a_log, gd_dt_bias, gd_head_norm, norm_ffn, ffn_conv, ffn_conv_b,
           norm_final, final_norm):
    n = x2.shape[0]
    nc = seq_len // GD_CHUNK
    w_main, w_gate, w_out, w_up, w_down = big
    gate_bias = jnp.concatenate(
        [ml_gate_bias, gd_dt_bias.reshape(-1), jnp.zeros((LANES - GATE_B,), F32)]).reshape(1, LANES)
    uv, qk, v, og, cq, z, gates = _inproj(
        x2, norm_mix.reshape(1, D_MODEL), w_main, w_gate, gate_bias, gd_conv, layer, seq_len, _TM)

    seq = lambda a: a.reshape(batch, seq_len, a.shape[-1])
    g_col = seq(gates)
    g32 = g_col[:, :, :N_GATES]
    g_rowp = g32.reshape(batch, nc, GD_CHUNK, 4, 2, _GD_PAIRS, 2).transpose(0, 1, 3, 4, 5, 6, 2)
    g_rowp = g_rowp.reshape(batch, nc, PAIR_ROWS, LANES)

    ws_cat = gm_ws.transpose(1, 0, 2).reshape(GM_CHUNK, GM_GROUPS * GM_CHUNK).astype(BF)
    bs_exp = jnp.repeat(gm_bs.T, GM_DIM, axis=1)
    ya = _gmlp(seq(uv), gm_norm.reshape(1, GM_WIDTH), ws_cat, bs_exp)

    yb = _mlstm(seq(qk), seq(v), seq(og), g_col, g_rowp, ml_head_norm.reshape(1, ML_V_W))

    a_flat = gd_a_log.reshape(-1)
    a_col = jnp.zeros((LANES,), F32).at[GATE_A:GATE_B].set(a_flat).reshape(1, LANES)
    a_rowp = jnp.zeros((PAIR_ROWS, LANES), F32).at[_GD_ROW_A:_GD_ROW_A + _GD_PD].set(
        jnp.repeat(a_flat.reshape(_GD_PD, 2), GD_K, axis=1))
    yc = _gdn(seq(cq), seq(z), g_col, g_rowp, a_col, a_rowp, gd_head_norm.reshape(1, GD_V_W))

    flat = lambda a: a.reshape(n, a.shape[-1])
    x2 = _outproj(x2, flat(ya), flat(yb), flat(yc), w_out, layer, _TM)
    return _ffn(x2, norm_ffn.reshape(1, D_MODEL), w_up, ffn_conv, ffn_conv_b.reshape(1, -1),
                w_down, norm_final.reshape(1, D_MODEL), layer, seq_len, _TM_FFN, final_norm)


def kernel(x, norm_mix, w_in, gm_norm, gm_ws, gm_bs, ml_gate_bias, ml_head_norm, gd_conv, gd_A_log,
           gd_dt_bias, gd_head_norm, w_out, norm_ffn, w_up, ffn_conv, ffn_conv_b, w_down, norm_final):
    batch, seq_len, _ = x.shape
    depth = w_in.shape[0]
    x2 = x.reshape(batch * seq_len, D_MODEL)
    big = _stacked_matmul_weights(w_in, w_out, w_up, w_down)
    for i in range(depth):
        x2 = _layer(x2, batch, seq_len, i, big, norm_mix[i], gm_norm[i], gm_ws[i], gm_bs[i],
                    ml_gate_bias[i], ml_head_norm[i], gd_conv[i], gd_A_log[i], gd_dt_bias[i],
                    gd_head_norm[i], norm_ffn[i], ffn_conv[i], ffn_conv_b[i],
                    norm_final, i == depth - 1)
    return x2.reshape(batch, seq_len, D_MODEL)
```

```python
import functools

import jax
import jax.numpy as jnp
from jax import lax
from jax.experimental import pallas as pl
from jax.experimental.pallas import tpu as pltpu

D_MODEL = 1024
GM_GROUPS = 4
GM_DIM = 64
GM_WIDTH = GM_GROUPS * GM_DIM
GM_CHUNK = 128
ML_HEADS = 4
ML_QK = 64
ML_V = 128
ML_QK_W = ML_HEADS * ML_QK
ML_V_W = ML_HEADS * ML_V
ML_CHUNK = 64
GD_HEADS = 4
GD_K = 64
GD_V = 64
GD_K_W = GD_HEADS * GD_K
GD_V_W = GD_HEADS * GD_V
GD_CHUNK = 64
GD_CONV = 5
D_FF = 2816
FFN_CONV = 3
EPS = 1e-6

LANES = 128
N_GATES = 32
GATE_I, GATE_F, GATE_A, GATE_B = 0, 8, 16, 24
PAIR_ROWS = 16
VMEM_LIMIT = 56 * 1024 * 1024

BF = jnp.bfloat16
F32 = jnp.float32


def _dot(a, b):
    return jnp.dot(a, b, preferred_element_type=F32)


def _dot_nt(a, b):
    return lax.dot_general(a, b, (((1,), (1,)), ((), ())), preferred_element_type=F32)


def _dot_tn(a, b):
    return lax.dot_general(a, b, (((0,), (0,)), ((), ())), preferred_element_type=F32)


def _split(x, parts):
    out = []
    for _ in range(parts - 1):
        hi = x.astype(BF)
        out.append(hi)
        x = x - hi.astype(F32)
    out.append(x.astype(BF))
    return out


def _sel_dot_rhs(sel, x, parts=3):
    acc = None
    for term in _split(x, parts):
        y = _dot(sel, term)
        acc = y if acc is None else acc + y
    return acc


def _sel_dot_lhs(x, sel, parts=3):
    acc = None
    for term in _split(x, parts):
        y = _dot(term, sel)
        acc = y if acc is None else acc + y
    return acc


def _expand_pair(xb, lane_lo):
    z = jnp.zeros((), xb.dtype)
    return jnp.concatenate([jnp.where(lane_lo, xb, z), jnp.where(lane_lo, z, xb)], axis=0)


def _run_interleaved(gens):
    results = [None] * len(gens)
    pending = list(range(len(gens)))
    while pending:
        for k in list(pending):
            try:
                next(gens[k])
            except StopIteration as stop:
                results[k] = stop.value
                pending.remove(k)
    return tuple(results)


def _running_max(x, reverse):
    n, w = x.shape
    sh = 1
    while sh < n:
        fill = jnp.full((sh, w), -jnp.inf, x.dtype)
        if reverse:
            shifted = jnp.concatenate([x[sh:], fill], axis=0)
        else:
            shifted = jnp.concatenate([fill, x[:n - sh]], axis=0)
        x = jnp.maximum(x, shifted)
        sh *= 2
    return x


def _pair_cumsum_mats():
    r = lax.broadcasted_iota(jnp.int32, (LANES, LANES), 0)
    c = lax.broadcasted_iota(jnp.int32, (LANES, LANES), 1)
    half = LANES // 2
    same = (r // half) == (c // half)
    prefix = jnp.where(same & (c >= r), 1.0, 0.0).astype(BF)
    suffix = jnp.where(same & (c <= r), 1.0, 0.0).astype(BF)
    return prefix, suffix


def _column_spread(first_col, n_cols, width):
    j = lax.broadcasted_iota(jnp.int32, (LANES, n_cols * width), 0)
    n = lax.broadcasted_iota(jnp.int32, (LANES, n_cols * width), 1) // width
    return jnp.where(j == first_col + n, 1.0, 0.0).astype(BF)


def _softplus(x):
    return jnp.maximum(x, 0.0) + jnp.log1p(jnp.exp(-jnp.abs(x)))


def _log_sigmoid(x):
    return jnp.minimum(x, 0.0) - jnp.log1p(jnp.exp(-jnp.abs(x)))


def _tri_masks(n):
    r = lax.broadcasted_iota(jnp.int32, (n, n), 0)
    c = lax.broadcasted_iota(jnp.int32, (n, n), 1)
    return c <= r, c >= r, c < r, c > r


def _group_ones(width, group):
    r = lax.broadcasted_iota(jnp.int32, (width, width), 0) // group
    c = lax.broadcasted_iota(jnp.int32, (width, width), 1) // group
    return jnp.where(r == c, 1.0, 0.0).astype(BF)


_IN_SEGS = ((0, 512), (512, 1024), (1024, 1536), (1536, 2048), (2048, 2816), (2816, 3072))
_IN_MAIN = 3072


_IN_HALO = 16
_IN_CONV_SEG = 4


def _inproj_kernel(x_ref, xp_ref, xn_ref, g_ref, w_ref, wg_ref, b_ref, cw_ref,
                   o_uv, o_qk, o_v, o_o, o_c, o_z, o_g, h_ref, c_ref, *, tiles_per_seq):
    tm = x_ref.shape[0]
    g = g_ref[...]
    pos = pl.program_id(0) % tiles_per_seq

    def norm(x):
        return x * lax.rsqrt(jnp.mean(x * x, axis=-1, keepdims=True) + EPS) * g

    h_ref[0:_IN_HALO, :] = jnp.where(pos == 0, 0.0, norm(xp_ref[...])).astype(BF)
    h_ref[_IN_HALO:_IN_HALO + tm, :] = norm(x_ref[...]).astype(BF)
    h_ref[_IN_HALO + tm:, :] = jnp.where(pos == tiles_per_seq - 1, 0.0, norm(xn_ref[...])).astype(BF)
    lo, hi = _IN_SEGS[_IN_CONV_SEG]
    c_ref[...] = _dot(h_ref[...], w_ref[:, lo:hi])
    h = h_ref[_IN_HALO:_IN_HALO + tm, :]
    for idx, o in enumerate((o_uv, o_qk, o_v, o_o, o_c, o_z)):
        if idx != _IN_CONV_SEG:
            lo, hi = _IN_SEGS[idx]
            o[...] = _dot(h, w_ref[:, lo:hi]).astype(o.dtype)
    o_g[...] = _dot(h, wg_ref[...]) + b_ref[...]

    cw = cw_ref[...]
    gsum = _group_ones(LANES, GD_K)
    pad = (GD_CONV - 1) // 2
    for s in range(_GD_QKV_W // LANES):
        sl = slice(s * LANES, (s + 1) * LANES)
        y = jnp.zeros((tm, LANES), F32)
        for tap in range(GD_CONV):
            r0 = _IN_HALO + tap - pad
            y = y + cw[tap:tap + 1, sl] * c_ref[r0:r0 + tm, sl]
        y = y * jax.nn.sigmoid(y)
        if s < 2 * GD_K_W // LANES:
            scale = GD_K ** -0.5 if s < GD_K_W // LANES else 1.0
            y = y * (lax.rsqrt(_sel_dot_lhs(y * y, gsum, parts=2) + EPS) * scale)
        o_c[:, sl] = y.astype(o_c.dtype)


def _layer_spec(shape, layer, **kw):
    zeros = (0,) * len(shape)
    return pl.BlockSpec((None,) + tuple(shape), lambda i: (layer,) + zeros, **kw)


def _inproj(x2, gain, w_main, w_gate, gate_bias, conv_w, layer, seq_len, tm):
    n = x2.shape[0]
    widths = [hi - lo for lo, hi in _IN_SEGS]
    out_shape = [jax.ShapeDtypeStruct((n, w), BF) for w in widths]
    out_shape.append(jax.ShapeDtypeStruct((n, LANES), F32))
    hb = tm // _IN_HALO
    nhb = n // _IN_HALO
    row = lambda i: (i, 0)
    const = lambda i: (0, 0)
    return pl.pallas_call(
        functools.partial(_inproj_kernel, tiles_per_seq=seq_len // tm),
        out_shape=out_shape,
        grid=(n // tm,),
        in_specs=[
            pl.BlockSpec((tm, D_MODEL), row),
            pl.BlockSpec((_IN_HALO, D_MODEL), lambda i: (jnp.maximum(i * hb - 1, 0), 0)),
            pl.BlockSpec((_IN_HALO, D_MODEL), lambda i: (jnp.minimum((i + 1) * hb, nhb - 1), 0)),
            pl.BlockSpec((1, D_MODEL), const),
            _layer_spec((D_MODEL, _IN_MAIN), layer),
            _layer_spec((D_MODEL, LANES), layer),
            pl.BlockSpec((1, LANES), const),
            pl.BlockSpec((GD_CONV, _GD_QKV_W), const),
        ],
        out_specs=[pl.BlockSpec((tm, w), row) for w in widths] + [pl.BlockSpec((tm, LANES), row)],
        scratch_shapes=[
            pltpu.VMEM((tm + 2 * _IN_HALO, D_MODEL), BF),
            pltpu.VMEM((tm + 2 * _IN_HALO, _GD_QKV_W), F32),
        ],
        compiler_params=pltpu.CompilerParams(
            dimension_semantics=("parallel",), vmem_limit_bytes=VMEM_LIMIT),
    )(x2, x2, x2, gain, w_main, w_gate, gate_bias, conv_w)


def _gmlp_kernel(uv_ref, gn_ref, ws_ref, bs_ref, y_ref):
    t = uv_ref.shape[0]
    gsum = _group_ones(GM_WIDTH, GM_DIM)
    rr = lax.broadcasted_iota(jnp.int32, (GM_GROUPS * GM_CHUNK, GM_WIDTH), 0) // GM_CHUNK
    cc = lax.broadcasted_iota(jnp.int32, (GM_GROUPS * GM_CHUNK, GM_WIDTH), 1) // GM_DIM
    blockdiag = rr == cc
    ws = ws_ref[...]
    bs = bs_ref[...]
    gn = gn_ref[...]

    def body(c, carry):
        off = pl.multiple_of(c * GM_CHUNK, GM_CHUNK)
        uv = uv_ref[pl.ds(off, GM_CHUNK), :].astype(F32)
        u = jax.nn.gelu(uv[:, :GM_WIDTH])
        v = jax.nn.gelu(uv[:, GM_WIDTH:])
        ms = _sel_dot_lhs(v * v, gsum, parts=2) * (1.0 / GM_DIM)
        vn = (v * lax.rsqrt(ms + EPS) * gn).astype(BF)
        vexp = jnp.where(blockdiag, jnp.concatenate([vn] * GM_GROUPS, axis=0), jnp.zeros((), BF))
        sg = _dot(ws, vexp) + bs
        y_ref[pl.ds(off, GM_CHUNK), :] = (u * sg).astype(y_ref.dtype)
        return carry

    lax.fori_loop(0, t // GM_CHUNK, body, 0, unroll=4)


def _gmlp(uv, gn, ws_cat, bs_exp):
    b, t, _ = uv.shape
    const = lambda i: (0, 0)
    return pl.pallas_call(
        _gmlp_kernel,
        out_shape=jax.ShapeDtypeStruct((b, t, GM_WIDTH), BF),
        grid=(b,),
        in_specs=[
            pl.BlockSpec((None, t, 2 * GM_WIDTH), lambda i: (i, 0, 0)),
            pl.BlockSpec((1, GM_WIDTH), const),
            pl.BlockSpec((GM_CHUNK, GM_GROUPS * GM_CHUNK), const),
            pl.BlockSpec((GM_CHUNK, GM_WIDTH), const),
        ],
        out_specs=pl.BlockSpec((None, t, GM_WIDTH), lambda i: (i, 0, 0)),
        compiler_params=pltpu.CompilerParams(
            dimension_semantics=("parallel",), vmem_limit_bytes=VMEM_LIMIT),
    )(uv, gn, ws_cat, bs_exp)


_ML_GROUP = 4


def _mlstm_kernel(qk_ref, v_ref, og_ref, gc_ref, gr_ref, hn_ref, y_ref,
                  acc_ref, bc_ref, cm_ref, kw_ref, st_ref):
    t = qk_ref.shape[0]
    L = ML_CHUNK
    nc = t // L
    lower, upper, _, _ = _tri_masks(L)
    lower_b = jnp.where(lower, 1.0, 0.0).astype(BF)
    upper_b = jnp.where(upper, 1.0, 0.0).astype(BF)
    ones_v = jnp.ones((L, ML_V), BF)
    zero_v = jnp.zeros((L, ML_V), BF)
    zero_c = jnp.zeros((ML_QK, ML_V), BF)
    tt = lax.broadcasted_iota(jnp.int32, (L, LANES), 0)
    ln = lax.broadcasted_iota(jnp.int32, (L, LANES), 1)
    lane_lo = ln < ML_QK
    mask = (ln % ML_QK <= tt, ln % ML_QK >= tt)
    f_cols = (ln >= GATE_F) & (ln < GATE_F + 2 * ML_HEADS)
    pair_prefix, pair_suffix = _pair_cumsum_mats()
    ro = lax.broadcasted_iota(jnp.int32, (LANES, 2 * ML_V), 0) // ML_QK
    co = lax.broadcasted_iota(jnp.int32, (LANES, 2 * ML_V), 1) // ML_V
    pair_rowsum = jnp.where(ro == co, 1.0, 0.0).astype(BF)
    bw_cols = (ln // ML_HEADS) % 2 == 1
    spread_k_all = _column_spread(GATE_F, 2 * ML_HEADS, ML_QK)

    def gate_stats(c):
        off = pl.multiple_of(c * L, L)
        gcol = gc_ref[pl.ds(off, L), :]
        grow = gr_ref[c]
        lf = _log_sigmoid(gcol)
        lf_r = _log_sigmoid(grow)
        bc = jnp.where(bw_cols, _sel_dot_rhs(upper_b, lf), _sel_dot_rhs(lower_b, lf))
        cum_f = _sel_dot_lhs(lf_r, pair_prefix)
        cum_b = _sel_dot_lhs(lf_r, pair_suffix)
        yield
        ig = pltpu.roll(gcol, GATE_F - GATE_I, 1)
        b_last = jnp.where(bw_cols[0:1], bc[0:1, :], bc[L - 1:L, :])
        a = b_last - bc + ig
        m_loc = jnp.max(a, axis=0, keepdims=True)
        wa_k = _sel_dot_lhs(jnp.where(f_cols, jnp.exp(a - m_loc), 0.0), spread_k_all, parts=1)
        yield
        r = ig - bc
        bc_ref[pl.ds(off, L), :] = bc
        cm_ref[pl.ds(off, L), :] = jnp.where(bw_cols, _running_max(r, True), _running_max(r, False))
        hp = ML_HEADS // 2
        rr = jnp.concatenate([grow[0:hp] - cum_f[ML_HEADS:ML_HEADS + hp],
                              grow[hp:2 * hp] - cum_b[ML_HEADS + hp:ML_HEADS + 2 * hp]], axis=0)
        st_ref[c] = jnp.concatenate([b_last, m_loc, rr, jnp.zeros((8 - 2 - 2 * hp, LANES), F32)], axis=0)
        yield
        for p in range(hp):
            k_p = qk_ref[pl.ds(off, L), ML_QK_W + p * LANES:ML_QK_W + (p + 1) * LANES].astype(F32)
            for d in range(2):
                lanes = slice((2 * d + p) * LANES, (2 * d + p + 1) * LANES)
                kw_ref[pl.ds(off, L), lanes] = (k_p * wa_k[:, lanes]).astype(BF)

    def gate_stats_group(j, carry):
        _run_interleaved([gate_stats(j * _ML_GROUP + k) for k in range(_ML_GROUP)])
        return carry

    lax.fori_loop(0, nc // _ML_GROUP, gate_stats_group, 0)

    def chunk_pre(cidx, d, m):
        off = pl.multiple_of(cidx * L, L)
        qkc = qk_ref[pl.ds(off, L), :]
        vc = v_ref[pl.ds(off, L), :]
        col0 = GATE_F + ML_HEADS * d
        spread_k = _column_spread(col0, ML_HEADS, ML_QK)
        spread_v = _column_spread(col0, ML_HEADS, ML_V)
        bc = bc_ref[pl.ds(off, L), :]
        stats = st_ref[cidx]
        b_last = stats[0:1, :]
        m_loc = stats[1:2, :]
        mx = jnp.maximum(m, cm_ref[pl.ds(off, L), :])
        m_new = jnp.maximum(b_last + m, m_loc)
        f_rows = jnp.concatenate(
            [jnp.exp(b_last + m - m_new), jnp.exp(m_loc - m_new), jnp.zeros((6, LANES), F32)], axis=0)
        clean = lambda x: jnp.where(f_cols[:x.shape[0]], x, 0.0)
        mx_k = _sel_dot_lhs(clean(mx), spread_k, parts=2)
        f_v = _sel_dot_lhs(clean(f_rows), spread_v)
        w_col = jnp.exp(m - mx)
        e_col = jnp.exp(-(bc + mx))

        def head_lanes(x, p):
            cols = [jnp.broadcast_to(x[:, col0 + 2 * p + hh:col0 + 2 * p + hh + 1], (L, ML_V)) for hh in range(2)]
            return jnp.concatenate(cols, axis=1)

        yield
        pairs = []
        for p in range(ML_HEADS // 2):
            row = 2 * d + p
            q_p = qkc[:, p * LANES:(p + 1) * LANES]
            k_p = qkc[:, ML_QK_W + p * LANES:ML_QK_W + (p + 1) * LANES]
            v0 = vc[:, 2 * p * ML_V:(2 * p + 1) * ML_V]
            v1 = vc[:, (2 * p + 1) * ML_V:(2 * p + 2) * ML_V]
            qk = _dot_nt(q_p, _expand_pair(k_p, lane_lo))
            rr = stats[2 + row:3 + row, :]
            s = qk * jnp.exp(jnp.where(mask[d], rr - mx_k[:, p * LANES:(p + 1) * LANES], -jnp.inf))
            rhs_s = jnp.concatenate(
                [jnp.concatenate([v0, zero_v], axis=1), jnp.concatenate([zero_v, v1], axis=1)], axis=0)
            intra = _dot(s.astype(BF), jnp.concatenate([rhs_s, pair_rowsum], axis=1))
            yield
            kw = kw_ref[pl.ds(off, L), row * LANES:(row + 1) * LANES]
            upd = _dot_tn(kw, jnp.concatenate([v0, v1, ones_v], axis=1))
            pairs.append(dict(q=q_p, intra=intra, upd=upd,
                              w2=head_lanes(w_col, p), e2=head_lanes(e_col, p)))
            yield
        return dict(d=d, off=off, pairs=pairs, f_v=f_v)

    def chunk_post(pre, heads):
        new_heads = []
        for p, pp in enumerate(pre["pairs"]):
            (c0, n0), (c1, n1) = heads[2 * p], heads[2 * p + 1]
            rhs_c = jnp.concatenate(
                [jnp.concatenate([c0.astype(BF), zero_c, n0.astype(BF), zero_c], axis=1),
                 jnp.concatenate([zero_c, c1.astype(BF), zero_c, n1.astype(BF)], axis=1)], axis=0)
            inter = _dot(pp["q"], rhs_c)
            na = jnp.concatenate([pp["w2"], pp["w2"]], axis=1) * inter + pp["intra"]
            out = na[:, :2 * ML_V] / jnp.maximum(jnp.abs(na[:, 2 * ML_V:]), pp["e2"])
            acc_ref[pre["d"], pl.ds(pre["off"], L), 2 * p * ML_V:(2 * p + 2) * ML_V] = out
            for hh, (c, n) in enumerate(((c0, n0), (c1, n1))):
                h = 2 * p + hh
                f_old = pre["f_v"][0:1, h * ML_V:(h + 1) * ML_V]
                f_loc = pre["f_v"][1:2, h * ML_V:(h + 1) * ML_V]
                rs = slice(hh * ML_QK, (hh + 1) * ML_QK)
                new_heads.append((f_old * c + f_loc * pp["upd"][rs, hh * ML_V:(hh + 1) * ML_V],
                                  f_old * n + f_loc * pp["upd"][rs, 2 * ML_V:3 * ML_V]))
        return tuple(new_heads)

    def body(j, carry):
        (heads_f, m_f), (heads_b, m_b) = carry
        chunks = [(j * _ML_GROUP + k, nc - 1 - (j * _ML_GROUP + k)) for k in range(_ML_GROUP)]
        ms_f, ms_b = [m_f], [m_b]
        for cf, cb in chunks:
            ms_f.append(jnp.maximum(st_ref[cf][0:1, :] + ms_f[-1], st_ref[cf][1:2, :]))
            ms_b.append(jnp.maximum(st_ref[cb][0:1, :] + ms_b[-1], st_ref[cb][1:2, :]))
        pres = _run_interleaved(
            [chunk_pre(cf, 0, ms_f[k]) for k, (cf, _) in enumerate(chunks)]
            + [chunk_pre(cb, 1, ms_b[k]) for k, (_, cb) in enumerate(chunks)])
        for k in range(_ML_GROUP):
            heads_f = chunk_post(pres[k], heads_f)
            heads_b = chunk_post(pres[_ML_GROUP + k], heads_b)
        return (heads_f, ms_f[-1]), (heads_b, ms_b[-1])

    zero_state = jnp.zeros((ML_QK, ML_V), F32)
    init = (tuple((zero_state, zero_state) for _ in range(ML_HEADS)), jnp.zeros((1, LANES), F32))
    lax.fori_loop(0, nc // _ML_GROUP, body, (init, init))

    hn = hn_ref[...]
    rows = 256

    def epilogue(j, carry):
        off = pl.multiple_of(j * rows, rows)
        hb = acc_ref[0, pl.ds(off, rows), :] + acc_ref[1, pl.ds(off, rows), :]
        gate = jax.nn.sigmoid(og_ref[pl.ds(off, rows), :].astype(F32))
        for h in range(ML_HEADS):
            sl = slice(h * ML_V, (h + 1) * ML_V)
            x = hb[:, sl]
            y = x * lax.rsqrt(jnp.mean(x * x, axis=-1, keepdims=True) + EPS) * hn[:, sl]
            y_ref[pl.ds(off, rows), sl] = (y * gate[:, sl]).astype(y_ref.dtype)
        return carry

    lax.fori_loop(0, t // rows, epilogue, 0, unroll=2)


def _mlstm(qk, v, og, g_col, g_row, hn):
    b, t, _ = qk.shape
    nc = t // ML_CHUNK
    seq = lambda i: (i, 0, 0)
    return pl.pallas_call(
        _mlstm_kernel,
        out_shape=jax.ShapeDtypeStruct((b, t, ML_V_W), BF),
        grid=(b,),
        in_specs=[
            pl.BlockSpec((None, t, 2 * ML_QK_W), seq),
            pl.BlockSpec((None, t, ML_V_W), seq),
            pl.BlockSpec((None, t, ML_V_W), seq),
            pl.BlockSpec((None, t, LANES), seq),
            pl.BlockSpec((None, nc, PAIR_ROWS, LANES), lambda i: (i, 0, 0, 0)),
            pl.BlockSpec((1, ML_V_W), lambda i: (0, 0)),
        ],
        out_specs=pl.BlockSpec((None, t, ML_V_W), seq),
        scratch_shapes=[
            pltpu.VMEM((2, t, ML_V_W), F32),
            pltpu.VMEM((t, LANES), F32),
            pltpu.VMEM((t, LANES), F32),
            pltpu.VMEM((t, 2 * ML_QK_W), BF),
            pltpu.VMEM((nc, 8, LANES), F32),
        ],
        compiler_params=pltpu.CompilerParams(
            dimension_semantics=("parallel",), vmem_limit_bytes=VMEM_LIMIT),
    )(qk, v, og, g_col, g_row, hn)


_GD_QKV_W = 2 * GD_K_W + GD_V_W
_GD_PAIRS = GD_HEADS // 2
_GD_PD = 2 * _GD_PAIRS
_GD_ROW_A = 8
_GD_A_CHUNKS = 4


def _gdn_kernel(qkv_ref, z_ref, gc_ref, gr_ref, acol_ref, arow_ref, hn_ref, y_ref,
                acc_ref, u_ref, wq_ref, att_ref, kd_ref, egl_ref):
    t = qkv_ref.shape[0]
    L = GD_CHUNK
    nc = t // L
    rows = 256
    incl_lo, incl_up, _, _ = _tri_masks(L)
    lower_b = jnp.where(incl_lo, 1.0, 0.0).astype(BF)
    upper_b = jnp.where(incl_up, 1.0, 0.0).astype(BF)

    tt = lax.broadcasted_iota(jnp.int32, (L, LANES), 0)
    ln = lax.broadcasted_iota(jnp.int32, (L, LANES), 1)
    ss = ln % GD_K
    lane_lo = ln < GD_K
    incl = (ss <= tt, ss >= tt)
    strict = (ss < tt, ss > tt)
    eye_pair = jnp.where(ss == tt, 1.0, 0.0)
    bw_cols = (ln // GD_HEADS) % 2 == 1
    r2 = lax.broadcasted_iota(jnp.int32, (2 * L, LANES), 0)
    c2 = lax.broadcasted_iota(jnp.int32, (2 * L, LANES), 1)
    same_half = (r2 // L) == (c2 // GD_K)
    bd_upper = jnp.where(same_half & (c2 % GD_K >= r2 % L), 1.0, 0.0).astype(BF)
    bd_lower = jnp.where(same_half & (c2 % GD_K <= r2 % L), 1.0, 0.0).astype(BF)
    ej = lax.broadcasted_iota(jnp.int32, (LANES, _GD_PD * LANES), 0)
    en = lax.broadcasted_iota(jnp.int32, (LANES, _GD_PD * LANES), 1) // GD_K
    e_a = jnp.where(ej == GATE_A + en, 1.0, 0.0).astype(BF)
    e_b = jnp.where(ej == GATE_B + en, 1.0, 0.0).astype(BF)
    a_col = jnp.exp(acol_ref[...])
    a_row = jnp.exp(arow_ref[...])

    def phase_a(chunk_ids):
        chains = []
        for cidx in chunk_ids:
            off = pl.multiple_of(cidx * L, L)
            gcol = gc_ref[pl.ds(off, L), :]
            grow = gr_ref[cidx]
            qkv = qkv_ref[pl.ds(off, L), :]
            g_c = -a_col * _softplus(gcol)
            cum_c = jnp.where(bw_cols, _sel_dot_rhs(upper_b, g_c), _sel_dot_rhs(lower_b, g_c))
            gcb_all = _sel_dot_lhs(cum_c, e_a, parts=2)
            beb_all = _sel_dot_lhs(jax.nn.sigmoid(gcol), e_b, parts=1)
            g_r = -a_row * _softplus(grow)
            cum_r = (_sel_dot_lhs(g_r, bd_upper), _sel_dot_lhs(g_r, bd_lower))
            for p in range(_GD_PAIRS):
                q_b = qkv[:, p * LANES:(p + 1) * LANES]
                k_b = qkv[:, GD_K_W + p * LANES:GD_K_W + (p + 1) * LANES]
                q_p = q_b.astype(F32)
                k_p = k_b.astype(F32)
                v_p = qkv[:, 2 * GD_K_W + p * LANES:2 * GD_K_W + (p + 1) * LANES].astype(F32)
                kexp = _expand_pair(k_b, lane_lo)
                kk = _dot_nt(k_b, kexp)
                qk = _dot_nt(q_b, kexp)
                for d in range(2):
                    pd = d * _GD_PAIRS + p
                    gcb = gcb_all[:, pd * LANES:(pd + 1) * LANES]
                    beb = beb_all[:, pd * LANES:(pd + 1) * LANES]
                    gcr = cum_r[d][_GD_ROW_A + pd:_GD_ROW_A + pd + 1, :]
                    decay = jnp.exp(jnp.where(incl[d], gcb - gcr, -jnp.inf))
                    lmat = jnp.where(strict[d], beb * kk * decay, 0.0)
                    eg = jnp.exp(gcb)
                    gl = gcb[L - 1:L, :] if d == 0 else gcb[0:1, :]
                    x = jnp.concatenate([v_p * beb, k_p * (beb * eg)], axis=1)
                    att_ref[pd, pl.ds(off, L), :] = (qk * decay).astype(BF)
                    wq_ref[pd, cidx, L:2 * L, :] = (q_p * eg).astype(BF)
                    kd_ref[pd, pl.ds(off, L), :] = (k_p * jnp.exp(gl - gcb)).astype(BF)
                    egl_ref[cidx, pd] = jnp.broadcast_to(jnp.exp(gl), (8, LANES))
                    chains.append(dict(pd=pd, cidx=cidx, off=off, p=lmat.astype(BF), l=lmat, x=x))
            yield

        for ch in chains:
            ch["t"] = eye_pair - ch["l"]
            ch["p"] = _dot(ch["p"], _expand_pair(ch["p"], lane_lo)).astype(BF)
        yield
        power = 2
        while power < L:
            last = 2 * power >= L
            for ch in chains:
                rhs = _expand_pair(ch["t"].astype(BF), lane_lo)
                if not last:
                    rhs = jnp.concatenate([_expand_pair(ch["p"], lane_lo), rhs], axis=1)
                ch["y"] = _dot(ch["p"], rhs)
            yield
            for ch in chains:
                if not last:
                    ch["p"] = ch["y"][:, :LANES].astype(BF)
                ch["t"] = ch["t"] + ch["y"][:, -LANES:]
            power *= 2
        for ch in chains:
            xb = ch["x"].astype(BF)
            rhs = jnp.concatenate([_expand_pair(xb[:, :LANES], lane_lo),
                                   _expand_pair(xb[:, LANES:], lane_lo)], axis=1)
            sol = _dot(ch["t"].astype(BF), rhs)
            u_ref[ch["pd"], pl.ds(ch["off"], L), :] = sol[:, :LANES]
            wq_ref[ch["pd"], ch["cidx"], 0:L, :] = sol[:, LANES:].astype(BF)

    rb = lax.broadcasted_iota(jnp.int32, (LANES, LANES), 0) // GD_K
    cb = lax.broadcasted_iota(jnp.int32, (LANES, LANES), 1) // GD_V
    blockdiag = rb == cb

    def phase_b(steps, states):
        loaded = []
        for i in steps:
            items = []
            for d in range(2):
                cidx = i if d == 0 else nc - 1 - i
                off = pl.multiple_of(cidx * L, L)
                for p in range(_GD_PAIRS):
                    pd = d * _GD_PAIRS + p
                    items.append(dict(
                        d=d, p=p, off=off, wq=wq_ref[pd, cidx], u=u_ref[pd, pl.ds(off, L), :],
                        kd=kd_ref[pd, pl.ds(off, L), :], att=att_ref[pd, pl.ds(off, L), :],
                        egl=egl_ref[cidx, pd][0:1, :]))
            loaded.append(items)
        yield
        for items in loaded:
            s_b = [s.astype(BF) for s in states]
            res = [_dot(it["wq"], s_b[pd]) for pd, it in enumerate(items)]
            yield
            v_new = [(it["u"] - res[pd][:L]).astype(BF) for pd, it in enumerate(items)]
            upd = [_dot_tn(it["kd"], v_new[pd]) for pd, it in enumerate(items)]
            yield
            states = tuple(states[pd] * it["egl"] + jnp.where(blockdiag, upd[pd], 0.0)
                           for pd, it in enumerate(items))
            outs = [res[pd][L:] + _dot(it["att"], _expand_pair(v_new[pd], lane_lo))
                    for pd, it in enumerate(items)]
            yield
            for it, o in zip(items, outs):
                acc_ref[it["d"], pl.ds(it["off"], L), it["p"] * LANES:(it["p"] + 1) * LANES] = o
        return states

    half = _GD_A_CHUNKS // 2
    n_groups = nc // _GD_A_CHUNKS

    def group_chunks(j):
        return [j * half + k for k in range(half)] + [nc - 1 - (j * half + k) for k in range(half)]

    def overlapped(j, states):
        steps = [(j - 1) * half + k for k in range(half)]
        return _run_interleaved([phase_b(steps, states), phase_a(group_chunks(j))])[0]

    hn = hn_ref[...]
    gsum_v = _group_ones(GD_V_W, GD_V)

    def finish_chunks(chunk_ids):
        loaded = []
        for c in chunk_ids:
            off = pl.multiple_of(c * L, L)
            loaded.append((off, acc_ref[0, pl.ds(off, L), :] + acc_ref[1, pl.ds(off, L), :],
                           z_ref[pl.ds(off, L), :]))
        yield
        for off, x, z in loaded:
            z = z.astype(F32)
            ms = _sel_dot_lhs(x * x, gsum_v, parts=2) * (1.0 / GD_V)
            yield
            y = x * lax.rsqrt(ms + EPS) * hn
            y_ref[pl.ds(off, L), :] = (y * (z * jax.nn.sigmoid(z))).astype(y_ref.dtype)
            yield

    def drain(i, states):
        return _run_interleaved([phase_b([i], states)])[0]

    def drain_finish(i, states):
        return _run_interleaved([phase_b([i], states), finish_chunks([i - 1, nc - i])])[0]

    first_finish = nc // 2 + 1
    _run_interleaved([phase_a(group_chunks(0))])
    states = tuple(jnp.zeros((LANES, LANES), F32) for _ in range(_GD_PD))
    states = lax.fori_loop(1, n_groups, overlapped, states)
    states = lax.fori_loop((n_groups - 1) * half, first_finish, drain, states, unroll=True)
    states = lax.fori_loop(first_finish, nc, drain_finish, states, unroll=3)
    _run_interleaved([finish_chunks([0, nc - 1])])


def _gdn(qkv, z, g_col, g_rowp, a_col, a_rowp, hn):
    b, t, _ = qkv.shape
    nc = t // GD_CHUNK
    seq = lambda i: (i, 0, 0)
    const = lambda i: (0, 0)
    return pl.pallas_call(
        _gdn_kernel,
        out_shape=jax.ShapeDtypeStruct((b, t, GD_V_W), BF),
        grid=(b,),
        in_specs=[
            pl.BlockSpec((None, t, _GD_QKV_W), seq),
            pl.BlockSpec((None, t, GD_V_W), seq),
            pl.BlockSpec((None, t, LANES), seq),
            pl.BlockSpec((None, nc, PAIR_ROWS, LANES), lambda i: (i, 0, 0, 0)),
            pl.BlockSpec((1, LANES), const),
            pl.BlockSpec((PAIR_ROWS, LANES), const),
            pl.BlockSpec((1, GD_V_W), const),
        ],
        out_specs=pl.BlockSpec((None, t, GD_V_W), seq),
        scratch_shapes=[
            pltpu.VMEM((2, t, GD_V_W), F32),
            pltpu.VMEM((_GD_PD, t, LANES), F32),
            pltpu.VMEM((_GD_PD, nc, 2 * GD_CHUNK, LANES), BF),
            pltpu.VMEM((_GD_PD, t, LANES), BF),
            pltpu.VMEM((_GD_PD, t, LANES), BF),
            pltpu.VMEM((nc, _GD_PD, 8, LANES), F32),
        ],
        compiler_params=pltpu.CompilerParams(
            dimension_semantics=("parallel",), vmem_limit_bytes=VMEM_LIMIT),
    )(qkv, z, g_col, g_rowp, a_col, a_rowp, hn)


def _outproj_kernel(x_ref, ya_ref, yb_ref, yc_ref, w_ref, o_ref):
    acc = x_ref[...]
    acc = acc + _dot(ya_ref[...], w_ref[0:GM_WIDTH, :])
    acc = acc + _dot(yb_ref[...], w_ref[GM_WIDTH:GM_WIDTH + ML_V_W, :])
    acc = acc + _dot(yc_ref[...], w_ref[GM_WIDTH + ML_V_W:, :])
    o_ref[...] = acc


def _outproj(x2, ya, yb, yc, w, layer, tm):
    n = x2.shape[0]
    row = lambda i: (i, 0)
    return pl.pallas_call(
        _outproj_kernel,
        out_shape=jax.ShapeDtypeStruct((n, D_MODEL), F32),
        grid=(n // tm,),
        in_specs=[
            pl.BlockSpec((tm, D_MODEL), row),
            pl.BlockSpec((tm, GM_WIDTH), row),
            pl.BlockSpec((tm, ML_V_W), row),
            pl.BlockSpec((tm, GD_V_W), row),
            _layer_spec((D_MODEL, D_MODEL), layer),
        ],
        out_specs=pl.BlockSpec((tm, D_MODEL), row),
        compiler_params=pltpu.CompilerParams(
            dimension_semantics=("parallel",), vmem_limit_bytes=VMEM_LIMIT),
    )(x2, ya, yb, yc, w)


_FFN_HALO = 16
_FFN_CHUNK = 256
_FFN_DOWN_GROUP = 4
_FFN_SLOTS = 3


def _ffn_kernel(xm_ref, xp_ref, xn_ref, g_ref, wup_ref, cw_ref, cb_ref, wdn_ref, gf_ref, o_ref,
                h_ref, act_ref, up_ref, *, tiles_per_seq, final_norm):
    i = pl.program_id(0)
    tm = xm_ref.shape[0]
    g = g_ref[...]
    pos = i % tiles_per_seq

    def norm(x):
        return x * lax.rsqrt(jnp.mean(x * x, axis=-1, keepdims=True) + EPS) * g

    hp = jnp.where(pos == 0, 0.0, norm(xp_ref[...]))
    hn = jnp.where(pos == tiles_per_seq - 1, 0.0, norm(xn_ref[...]))
    xm = xm_ref[...]
    h_ref[0:_FFN_HALO, :] = hp.astype(BF)
    h_ref[_FFN_HALO:_FFN_HALO + tm, :] = norm(xm).astype(BF)
    h_ref[_FFN_HALO + tm:, :] = hn.astype(BF)
    hh = h_ref[...]
    cw = cw_ref[...]
    cb = cb_ref[...]
    o_ref[...] = xm

    def up_project(j):
        c0 = j * _FFN_CHUNK
        slot = j % _FFN_SLOTS
        up_ref[slot, :, 0:_FFN_CHUNK] = _dot(hh, wup_ref[:, c0:c0 + _FFN_CHUNK])
        up_ref[slot, :, _FFN_CHUNK:] = _dot(hh, wup_ref[:, D_FF + c0:D_FF + c0 + _FFN_CHUNK])

    def conv(slot, lanes, c0):
        sl = slice(c0, c0 + _FFN_CHUNK)
        return (cw[0:1, sl] * up_ref[slot, _FFN_HALO - 1:_FFN_HALO - 1 + tm, lanes]
                + cw[1:2, sl] * up_ref[slot, _FFN_HALO:_FFN_HALO + tm, lanes]
                + cw[2:3, sl] * up_ref[slot, _FFN_HALO + 1:_FFN_HALO + 1 + tm, lanes] + cb[:, sl])

    n_chunks = D_FF // _FFN_CHUNK
    for j in range(_FFN_SLOTS - 1):
        up_project(j)
    for j in range(n_chunks):
        if j + _FFN_SLOTS - 1 < n_chunks:
            up_project(j + _FFN_SLOTS - 1)
        c0 = j * _FFN_CHUNK
        gt = conv(j % _FFN_SLOTS, slice(0, _FFN_CHUNK), c0)
        vl = conv(j % _FFN_SLOTS, slice(_FFN_CHUNK, 2 * _FFN_CHUNK), D_FF + c0)
        act_ref[:, c0:c0 + _FFN_CHUNK] = (gt * jax.nn.sigmoid(gt) * vl).astype(BF)
        if (j + 1) % _FFN_DOWN_GROUP == 0 or j + 1 == n_chunks:
            k0 = (j // _FFN_DOWN_GROUP) * _FFN_DOWN_GROUP * _FFN_CHUNK
            k1 = c0 + _FFN_CHUNK
            o_ref[...] += _dot(act_ref[:, k0:k1], wdn_ref[k0:k1, :])

    if final_norm:
        y = o_ref[...]
        o_ref[...] = y * lax.rsqrt(jnp.mean(y * y, axis=-1, keepdims=True) + EPS) * gf_ref[...]


def _ffn(x2, gain, w_up, conv_w, conv_b, w_down, gain_final, layer, seq_len, tm, final_norm):
    n = x2.shape[0]
    hb = tm // _FFN_HALO
    nhb = n // _FFN_HALO
    row = lambda i: (i, 0)
    const = lambda i: (0, 0)
    kern = functools.partial(_ffn_kernel, tiles_per_seq=seq_len // tm, final_norm=final_norm)
    return pl.pallas_call(
        kern,
        out_shape=jax.ShapeDtypeStruct((n, D_MODEL), F32),
        grid=(n // tm,),
        in_specs=[
            pl.BlockSpec((tm, D_MODEL), row),
            pl.BlockSpec((_FFN_HALO, D_MODEL), lambda i: (jnp.maximum(i * hb - 1, 0), 0)),
            pl.BlockSpec((_FFN_HALO, D_MODEL), lambda i: (jnp.minimum((i + 1) * hb, nhb - 1), 0)),
            pl.BlockSpec((1, D_MODEL), const),
            _layer_spec((D_MODEL, 2 * D_FF), layer, pipeline_mode=pl.Buffered(1)),
            pl.BlockSpec((FFN_CONV, 2 * D_FF), const),
            pl.BlockSpec((1, 2 * D_FF), const),
            _layer_spec((D_FF, D_MODEL), layer, pipeline_mode=pl.Buffered(1)),
            pl.BlockSpec((1, D_MODEL), const),
        ],
        out_specs=pl.BlockSpec((tm, D_MODEL), row),
        scratch_shapes=[
            pltpu.VMEM((tm + 2 * _FFN_HALO, D_MODEL), BF),
            pltpu.VMEM((tm, D_FF), BF),
            pltpu.VMEM((_FFN_SLOTS, tm + 2 * _FFN_HALO, 2 * _FFN_CHUNK), F32),
        ],
        compiler_params=pltpu.CompilerParams(
            dimension_semantics=("parallel",), vmem_limit_bytes=VMEM_LIMIT),
    )(x2, x2, x2, gain, w_up, conv_w, conv_b, w_down, gain_final)


_TM = 512
_TM_FFN = 1024


def _stacked_matmul_weights(w_in, w_out, w_up, w_down):
    depth = w_in.shape[0]
    g0 = 2 * GM_WIDTH + 2 * ML_QK_W + 2 * ML_V_W
    g1 = g0 + 4 * ML_HEADS
    c1 = g1 + _GD_QKV_W + GD_V_W
    q0 = 2 * GM_WIDTH
    col_scale = jnp.ones((_IN_MAIN,), F32).at[q0:q0 + ML_QK_W].set(ML_QK ** -0.5)
    w_main = (jnp.concatenate([w_in[:, :, :g0], w_in[:, :, g1:c1]], axis=2) * col_scale).astype(BF)
    w_gate = jnp.concatenate(
        [w_in[:, :, g0:g1], w_in[:, :, c1:], jnp.zeros((depth, D_MODEL, LANES - N_GATES), F32)],
        axis=2).astype(BF)
    return w_main, w_gate, w_out.astype(BF), w_up.astype(BF), w_down.astype(BF)


def _layer(x2, batch, seq_len, layer, big, norm_mix, gm_norm, gm_ws, gm_bs, ml_gate_bias, ml_head_norm,
           gd_conv, gd_a_log, gd_dt_bias, gd_head_norm, norm_ffn, ffn_conv, ffn_conv_b,
           norm_final, final_norm):
    n = x2.shape[0]
    nc = seq_len // GD_CHUNK
    w_main, w_gate, w_out, w_up, w_down = big
    gate_bias = jnp.concatenate(
        [ml_gate_bias, gd_dt_bias.reshape(-1), jnp.zeros((LANES - GATE_B,), F32)]).reshape(1, LANES)
    uv, qk, v, og, cq, z, gates = _inproj(
        x2, norm_mix.reshape(1, D_MODEL), w_main, w_gate, gate_bias, gd_conv, layer, seq_len, _TM)

    seq = lambda a: a.reshape(batch, seq_len, a.shape[-1])
    g_col = seq(gates)
    g32 = g_col[:, :, :N_GATES]
    g_rowp = g32.reshape(batch, nc, GD_CHUNK, 4, 2, _GD_PAIRS, 2).transpose(0, 1, 3, 4, 5, 6, 2)
    g_rowp = g_rowp.reshape(batch, nc, PAIR_ROWS, LANES)

    ws_cat = gm_ws.transpose(1, 0, 2).reshape(GM_CHUNK, GM_GROUPS * GM_CHUNK).astype(BF)
    bs_exp = jnp.repeat(gm_bs.T, GM_DIM, axis=1)
    ya = _gmlp(seq(uv), gm_norm.reshape(1, GM_WIDTH), ws_cat, bs_exp)

    yb = _mlstm(seq(qk), seq(v), seq(og), g_col, g_rowp, ml_head_norm.reshape(1, ML_V_W))

    a_flat = gd_a_log.reshape(-1)
    a_col = jnp.zeros((LANES,), F32).at[GATE_A:GATE_B].set(a_flat).reshape(1, LANES)
    a_rowp = jnp.zeros((PAIR_ROWS, LANES), F32).at[_GD_ROW_A:_GD_ROW_A + _GD_PD].set(
        jnp.repeat(a_flat.reshape(_GD_PD, 2), GD_K, axis=1))
    yc = _gdn(seq(cq), seq(z), g_col, g_rowp, a_col, a_rowp, gd_head_norm.reshape(1, GD_V_W))

    flat = lambda a: a.reshape(n, a.shape[-1])
    x2 = _outproj(x2, flat(ya), flat(yb), flat(yc), w_out, layer, _TM)
    return _ffn(x2, norm_ffn.reshape(1, D_MODEL), w_up, ffn_conv, ffn_conv_b.reshape(1, -1),
                w_down, norm_final.reshape(1, D_MODEL), layer, seq_len, _TM_FFN, final_norm)


def kernel(x, norm_mix, w_in, gm_norm, gm_ws, gm_bs, ml_gate_bias, ml_head_norm, gd_conv, gd_A_log,
           gd_dt_bias, gd_head_norm, w_out, norm_ffn, w_up, ffn_conv, ffn_conv_b, w_down, norm_final):
    batch, seq_len, _ = x.shape
    depth = w_in.shape[0]
    x2 = x.reshape(batch * seq_len, D_MODEL)
    big = _stacked_matmul_weights(w_in, w_out, w_up, w_down)
    for i in range(depth):
        x2 = _layer(x2, batch, seq_len, i, big, norm_mix[i], gm_norm[i], gm_ws[i], gm_bs[i],
                    ml_gate_bias[i], ml_head_norm[i], gd_conv[i], gd_A_log[i], gd_dt_bias[i],
                    gd_head_norm[i], norm_ffn[i], ffn_conv[i], ffn_conv_b[i],
                    norm_final, i == depth - 1)
    return x2.reshape(batch, seq_len, D_MODEL)
```

```python
import functools

import jax
import jax.numpy as jnp
from jax import lax
from jax.experimental import pallas as pl
from jax.experimental.pallas import tpu as pltpu

D_MODEL = 1024
GM_GROUPS = 4
GM_DIM = 64
GM_WIDTH = GM_GROUPS * GM_DIM
GM_CHUNK = 128
ML_HEADS = 4
ML_QK = 64
ML_V = 128
ML_QK_W = ML_HEADS * ML_QK
ML_V_W = ML_HEADS * ML_V
ML_CHUNK = 64
GD_HEADS = 4
GD_K = 64
GD_V = 64
GD_K_W = GD_HEADS * GD_K
GD_V_W = GD_HEADS * GD_V
GD_CHUNK = 64
GD_CONV = 5
D_FF = 2816
FFN_CONV = 3
EPS = 1e-6

LANES = 128
N_GATES = 32
GATE_I, GATE_F, GATE_A, GATE_B = 0, 8, 16, 24
PAIR_ROWS = 16
VMEM_LIMIT = 56 * 1024 * 1024

BF = jnp.bfloat16
F32 = jnp.float32


def _dot(a, b):
    return jnp.dot(a, b, preferred_element_type=F32)


def _dot_nt(a, b):
    return lax.dot_general(a, b, (((1,), (1,)), ((), ())), preferred_element_type=F32)


def _dot_tn(a, b):
    return lax.dot_general(a, b, (((0,), (0,)), ((), ())), preferred_element_type=F32)


def _split(x, parts):
    out = []
    for _ in range(parts - 1):
        hi = x.astype(BF)
        out.append(hi)
        x = x - hi.astype(F32)
    out.append(x.astype(BF))
    return out


def _sel_dot_rhs(sel, x, parts=3):
    acc = None
    for term in _split(x, parts):
        y = _dot(sel, term)
        acc = y if acc is None else acc + y
    return acc


def _sel_dot_lhs(x, sel, parts=3):
    acc = None
    for term in _split(x, parts):
        y = _dot(term, sel)
        acc = y if acc is None else acc + y
    return acc


def _expand_pair(xb, lane_lo):
    z = jnp.zeros((), xb.dtype)
    return jnp.concatenate([jnp.where(lane_lo, xb, z), jnp.where(lane_lo, z, xb)], axis=0)


def _run_interleaved(gens):
    results = [None] * len(gens)
    pending = list(range(len(gens)))
    while pending:
        for k in list(pending):
            try:
                next(gens[k])
            except StopIteration as stop:
                results[k] = stop.value
                pending.remove(k)
    return tuple(results)


def _running_max(x, reverse):
    n, w = x.shape
    sh = 1
    while sh < n:
        fill = jnp.full((sh, w), -jnp.inf, x.dtype)
        if reverse:
            shifted = jnp.concatenate([x[sh:], fill], axis=0)
        else:
            shifted = jnp.concatenate([fill, x[:n - sh]], axis=0)
        x = jnp.maximum(x, shifted)
        sh *= 2
    return x


def _pair_cumsum_mats():
    r = lax.broadcasted_iota(jnp.int32, (LANES, LANES), 0)
    c = lax.broadcasted_iota(jnp.int32, (LANES, LANES), 1)
    half = LANES // 2
    same = (r // half) == (c // half)
    prefix = jnp.where(same & (c >= r), 1.0, 0.0).astype(BF)
    suffix = jnp.where(same & (c <= r), 1.0, 0.0).astype(BF)
    return prefix, suffix


def _column_spread(first_col, n_cols, width):
    j = lax.broadcasted_iota(jnp.int32, (LANES, n_cols * width), 0)
    n = lax.broadcasted_iota(jnp.int32, (LANES, n_cols * width), 1) // width
    return jnp.where(j == first_col + n, 1.0, 0.0).astype(BF)


def _softplus(x):
    return jnp.maximum(x, 0.0) + jnp.log1p(jnp.exp(-jnp.abs(x)))


def _log_sigmoid(x):
    return jnp.minimum(x, 0.0) - jnp.log1p(jnp.exp(-jnp.abs(x)))


def _tri_masks(n):
    r = lax.broadcasted_iota(jnp.int32, (n, n), 0)
    c = lax.broadcasted_iota(jnp.int32, (n, n), 1)
    return c <= r, c >= r, c < r, c > r


def _group_ones(width, group):
    r = lax.broadcasted_iota(jnp.int32, (width, width), 0) // group
    c = lax.broadcasted_iota(jnp.int32, (width, width), 1) // group
    return jnp.where(r == c, 1.0, 0.0).astype(BF)


_IN_SEGS = ((0, 512), (512, 1024), (1024, 1536), (1536, 2048), (2048, 2816), (2816, 3072))
_IN_MAIN = 3072


_IN_HALO = 16
_IN_CONV_SEG = 4


def _inproj_kernel(x_ref, xp_ref, xn_ref, g_ref, w_ref, wg_ref, b_ref, cw_ref,
                   o_uv, o_qk, o_v, o_o, o_c, o_z, o_g, h_ref, c_ref, *, tiles_per_seq):
    tm = x_ref.shape[0]
    g = g_ref[...]
    pos = pl.program_id(0) % tiles_per_seq

    def norm(x):
        return x * lax.rsqrt(jnp.mean(x * x, axis=-1, keepdims=True) + EPS) * g

    h_ref[0:_IN_HALO, :] = jnp.where(pos == 0, 0.0, norm(xp_ref[...])).astype(BF)
    h_ref[_IN_HALO:_IN_HALO + tm, :] = norm(x_ref[...]).astype(BF)
    h_ref[_IN_HALO + tm:, :] = jnp.where(pos == tiles_per_seq - 1, 0.0, norm(xn_ref[...])).astype(BF)
    lo, hi = _IN_SEGS[_IN_CONV_SEG]
    c_ref[...] = _dot(h_ref[...], w_ref[:, lo:hi])
    h = h_ref[_IN_HALO:_IN_HALO + tm, :]
    for idx, o in enumerate((o_uv, o_qk, o_v, o_o, o_c, o_z)):
        if idx != _IN_CONV_SEG:
            lo, hi = _IN_SEGS[idx]
            o[...] = _dot(h, w_ref[:, lo:hi]).astype(o.dtype)
    o_g[...] = _dot(h, wg_ref[...]) + b_ref[...]

    cw = cw_ref[...]
    gsum = _group_ones(LANES, GD_K)
    pad = (GD_CONV - 1) // 2
    for s in range(_GD_QKV_W // LANES):
        sl = slice(s * LANES, (s + 1) * LANES)
        y = jnp.zeros((tm, LANES), F32)
        for tap in range(GD_CONV):
            r0 = _IN_HALO + tap - pad
            y = y + cw[tap:tap + 1, sl] * c_ref[r0:r0 + tm, sl]
        y = y * jax.nn.sigmoid(y)
        if s < 2 * GD_K_W // LANES:
            scale = GD_K ** -0.5 if s < GD_K_W // LANES else 1.0
            y = y * (lax.rsqrt(_sel_dot_lhs(y * y, gsum, parts=2) + EPS) * scale)
        o_c[:, sl] = y.astype(o_c.dtype)


def _layer_spec(shape, layer, **kw):
    zeros = (0,) * len(shape)
    return pl.BlockSpec((None,) + tuple(shape), lambda i: (layer,) + zeros, **kw)


def _inproj(x2, gain, w_main, w_gate, gate_bias, conv_w, layer, seq_len, tm):
    n = x2.shape[0]
    widths = [hi - lo for lo, hi in _IN_SEGS]
    out_shape = [jax.ShapeDtypeStruct((n, w), BF) for w in widths]
    out_shape.append(jax.ShapeDtypeStruct((n, LANES), F32))
    hb = tm // _IN_HALO
    nhb = n // _IN_HALO
    row = lambda i: (i, 0)
    const = lambda i: (0, 0)
    return pl.pallas_call(
        functools.partial(_inproj_kernel, tiles_per_seq=seq_len // tm),
        out_shape=out_shape,
        grid=(n // tm,),
        in_specs=[
            pl.BlockSpec((tm, D_MODEL), row),
            pl.BlockSpec((_IN_HALO, D_MODEL), lambda i: (jnp.maximum(i * hb - 1, 0), 0)),
            pl.BlockSpec((_IN_HALO, D_MODEL), lambda i: (jnp.minimum((i + 1) * hb, nhb - 1), 0)),
            pl.BlockSpec((1, D_MODEL), const),
            _layer_spec((D_MODEL, _IN_MAIN), layer),
            _layer_spec((D_MODEL, LANES), layer),
            pl.BlockSpec((1, LANES), const),
            pl.BlockSpec((GD_CONV, _GD_QKV_W), const),
        ],
        out_specs=[pl.BlockSpec((tm, w), row) for w in widths] + [pl.BlockSpec((tm, LANES), row)],
        scratch_shapes=[
            pltpu.VMEM((tm + 2 * _IN_HALO, D_MODEL), BF),
            pltpu.VMEM((tm + 2 * _IN_HALO, _GD_QKV_W), F32),
        ],
        compiler_params=pltpu.CompilerParams(
            dimension_semantics=("parallel",), vmem_limit_bytes=VMEM_LIMIT),
    )(x2, x2, x2, gain, w_main, w_gate, gate_bias, conv_w)


def _gmlp_kernel(uv_ref, gn_ref, ws_ref, bs_ref, y_ref):
    t = uv_ref.shape[0]
    gsum = _group_ones(GM_WIDTH, GM_DIM)
    rr = lax.broadcasted_iota(jnp.int32, (GM_GROUPS * GM_CHUNK, GM_WIDTH), 0) // GM_CHUNK
    cc = lax.broadcasted_iota(jnp.int32, (GM_GROUPS * GM_CHUNK, GM_WIDTH), 1) // GM_DIM
    blockdiag = rr == cc
    ws = ws_ref[...]
    bs = bs_ref[...]
    gn = gn_ref[...]

    def body(c, carry):
        off = pl.multiple_of(c * GM_CHUNK, GM_CHUNK)
        uv = uv_ref[pl.ds(off, GM_CHUNK), :].astype(F32)
        u = jax.nn.gelu(uv[:, :GM_WIDTH])
        v = jax.nn.gelu(uv[:, GM_WIDTH:])
        ms = _sel_dot_lhs(v * v, gsum, parts=2) * (1.0 / GM_DIM)
        vn = (v * lax.rsqrt(ms + EPS) * gn).astype(BF)
        vexp = jnp.where(blockdiag, jnp.concatenate([vn] * GM_GROUPS, axis=0), jnp.zeros((), BF))
        sg = _dot(ws, vexp) + bs
        y_ref[pl.ds(off, GM_CHUNK), :] = (u * sg).astype(y_ref.dtype)
        return carry

    lax.fori_loop(0, t // GM_CHUNK, body, 0, unroll=4)


def _gmlp(uv, gn, ws_cat, bs_exp):
    b, t, _ = uv.shape
    const = lambda i: (0, 0)
    return pl.pallas_call(
        _gmlp_kernel,
        out_shape=jax.ShapeDtypeStruct((b, t, GM_WIDTH), BF),
        grid=(b,),
        in_specs=[
            pl.BlockSpec((None, t, 2 * GM_WIDTH), lambda i: (i, 0, 0)),
            pl.BlockSpec((1, GM_WIDTH), const),
            pl.BlockSpec((GM_CHUNK, GM_GROUPS * GM_CHUNK), const),
            pl.BlockSpec((GM_CHUNK, GM_WIDTH), const),
        ],
        out_specs=pl.BlockSpec((None, t, GM_WIDTH), lambda i: (i, 0, 0)),
        compiler_params=pltpu.CompilerParams(
            dimension_semantics=("parallel",), vmem_limit_bytes=VMEM_LIMIT),
    )(uv, gn, ws_cat, bs_exp)


_ML_GROUP = 4


def _mlstm_kernel(qk_ref, v_ref, og_ref, gc_ref, gr_ref, hn_ref, y_ref,
                  acc_ref, bc_ref, cm_ref, kw_ref, st_ref):
    t = qk_ref.shape[0]
    L = ML_CHUNK
    nc = t // L
    lower, upper, _, _ = _tri_masks(L)
    lower_b = jnp.where(lower, 1.0, 0.0).astype(BF)
    upper_b = jnp.where(upper, 1.0, 0.0).astype(BF)
    ones_v = jnp.ones((L, ML_V), BF)
    zero_v = jnp.zeros((L, ML_V), BF)
    zero_c = jnp.zeros((ML_QK, ML_V), BF)
    tt = lax.broadcasted_iota(jnp.int32, (L, LANES), 0)
    ln = lax.broadcasted_iota(jnp.int32, (L, LANES), 1)
    lane_lo = ln < ML_QK
    mask = (ln % ML_QK <= tt, ln % ML_QK >= tt)
    f_cols = (ln >= GATE_F) & (ln < GATE_F + 2 * ML_HEADS)
    pair_prefix, pair_suffix = _pair_cumsum_mats()
    ro = lax.broadcasted_iota(jnp.int32, (LANES, 2 * ML_V), 0) // ML_QK
    co = lax.broadcasted_iota(jnp.int32, (LANES, 2 * ML_V), 1) // ML_V
    pair_rowsum = jnp.where(ro == co, 1.0, 0.0).astype(BF)
    bw_cols = (ln // ML_HEADS) % 2 == 1
    spread_k_all = _column_spread(GATE_F, 2 * ML_HEADS, ML_QK)

    def gate_stats(c):
        off = pl.multiple_of(c * L, L)
        gcol = gc_ref[pl.ds(off, L), :]
        grow = gr_ref[c]
        lf = _log_sigmoid(gcol)
        lf_r = _log_sigmoid(grow)
        bc = jnp.where(bw_cols, _sel_dot_rhs(upper_b, lf), _sel_dot_rhs(lower_b, lf))
        cum_f = _sel_dot_lhs(lf_r, pair_prefix)
        cum_b = _sel_dot_lhs(lf_r, pair_suffix)
        yield
        ig = pltpu.roll(gcol, GATE_F - GATE_I, 1)
        b_last = jnp.where(bw_cols[0:1], bc[0:1, :], bc[L - 1:L, :])
        a = b_last - bc + ig
        m_loc = jnp.max(a, axis=0, keepdims=True)
        wa_k = _sel_dot_lhs(jnp.where(f_cols, jnp.exp(a - m_loc), 0.0), spread_k_all, parts=1)
        yield
        r = ig - bc
        bc_ref[pl.ds(off, L), :] = bc
        cm_ref[pl.ds(off, L), :] = jnp.where(bw_cols, _running_max(r, True), _running_max(r, False))
        hp = ML_HEADS // 2
        rr = jnp.concatenate([grow[0:hp] - cum_f[ML_HEADS:ML_HEADS + hp],
                              grow[hp:2 * hp] - cum_b[ML_HEADS + hp:ML_HEADS + 2 * hp]], axis=0)
        st_ref[c] = jnp.concatenate([b_last, m_loc, rr, jnp.zeros((8 - 2 - 2 * hp, LANES), F32)], axis=0)
        yield
        for p in range(hp):
            k_p = qk_ref[pl.ds(off, L), ML_QK_W + p * LANES:ML_QK_W + (p + 1) * LANES].astype(F32)
            for d in range(2):
                lanes = slice((2 * d + p) * LANES, (2 * d + p + 1) * LANES)
                kw_ref[pl.ds(off, L), lanes] = (k_p * wa_k[:, lanes]).astype(BF)

    def gate_stats_group(j, carry):
        _run_interleaved([gate_stats(j * _ML_GROUP + k) for k in range(_ML_GROUP)])
        return carry

    lax.fori_loop(0, nc // _ML_GROUP, gate_stats_group, 0)

    def chunk_pre(cidx, d, m):
        off = pl.multiple_of(cidx * L, L)
        qkc = qk_ref[pl.ds(off, L), :]
        vc = v_ref[pl.ds(off, L), :]
        col0 = GATE_F + ML_HEADS * d
        spread_k = _column_spread(col0, ML_HEADS, ML_QK)
        spread_v = _column_spread(col0, ML_HEADS, ML_V)
        bc = bc_ref[pl.ds(off, L), :]
        stats = st_ref[cidx]
        b_last = stats[0:1, :]
        m_loc = stats[1:2, :]
        mx = jnp.maximum(m, cm_ref[pl.ds(off, L), :])
        m_new = jnp.maximum(b_last + m, m_loc)
        f_rows = jnp.concatenate(
            [jnp.exp(b_last + m - m_new), jnp.exp(m_loc - m_new), jnp.zeros((6, LANES), F32)], axis=0)
        clean = lambda x: jnp.where(f_cols[:x.shape[0]], x, 0.0)
        mx_k = _sel_dot_lhs(clean(mx), spread_k, parts=2)
        f_v = _sel_dot_lhs(clean(f_rows), spread_v)
        w_col = jnp.exp(m - mx)
        e_col = jnp.exp(-(bc + mx))

        def head_lanes(x, p):
            cols = [jnp.broadcast_to(x[:, col0 + 2 * p + hh:col0 + 2 * p + hh + 1], (L, ML_V)) for hh in range(2)]
            return jnp.concatenate(cols, axis=1)

        yield
        pairs = []
        for p in range(ML_HEADS // 2):
            row = 2 * d + p
            q_p = qkc[:, p * LANES:(p + 1) * LANES]
            k_p = qkc[:, ML_QK_W + p * LANES:ML_QK_W + (p + 1) * LANES]
            v0 = vc[:, 2 * p * ML_V:(2 * p + 1) * ML_V]
            v1 = vc[:, (2 * p + 1) * ML_V:(2 * p + 2) * ML_V]
            qk = _dot_nt(q_p, _expand_pair(k_p, lane_lo))
            rr = stats[2 + row:3 + row, :]
            s = qk * jnp.exp(jnp.where(mask[d], rr - mx_k[:, p * LANES:(p + 1) * LANES], -jnp.inf))
            rhs_s = jnp.concatenate(
                [jnp.concatenate([v0, zero_v], axis=1), jnp.concatenate([zero_v, v1], axis=1)], axis=0)
            intra = _dot(s.astype(BF), jnp.concatenate([rhs_s, pair_rowsum], axis=1))
            yield
            kw = kw_ref[pl.ds(off, L), row * LANES:(row + 1) * LANES]
            upd = _dot_tn(kw, jnp.concatenate([v0, v1, ones_v], axis=1))
            pairs.append(dict(q=q_p, intra=intra, upd=upd,
                              w2=head_lanes(w_col, p), e2=head_lanes(e_col, p)))
            yield
        return dict(d=d, off=off, pairs=pairs, f_v=f_v)

    def chunk_post(pre, heads):
        new_heads = []
        for p, pp in enumerate(pre["pairs"]):
            (c0, n0), (c1, n1) = heads[2 * p], heads[2 * p + 1]
            rhs_c = jnp.concatenate(
                [jnp.concatenate([c0.astype(BF), zero_c, n0.astype(BF), zero_c], axis=1),
                 jnp.concatenate([zero_c, c1.astype(BF), zero_c, n1.astype(BF)], axis=1)], axis=0)
            inter = _dot(pp["q"], rhs_c)
            na = jnp.concatenate([pp["w2"], pp["w2"]], axis=1) * inter + pp["intra"]
            out = na[:, :2 * ML_V] / jnp.maximum(jnp.abs(na[:, 2 * ML_V:]), pp["e2"])
            acc_ref[pre["d"], pl.ds(pre["off"], L), 2 * p * ML_V:(2 * p + 2) * ML_V] = out
            for hh, (c, n) in enumerate(((c0, n0), (c1, n1))):
                h = 2 * p + hh
                f_old = pre["f_v"][0:1, h * ML_V:(h + 1) * ML_V]
                f_loc = pre["f_v"][1:2, h * ML_V:(h + 1) * ML_V]
                rs = slice(hh * ML_QK, (hh + 1) * ML_QK)
                new_heads.append((f_old * c + f_loc * pp["upd"][rs, hh * ML_V:(hh + 1) * ML_V],
                                  f_old * n + f_loc * pp["upd"][rs, 2 * ML_V:3 * ML_V]))
        return tuple(new_heads)

    hn = hn_ref[...]
    rows = _ML_GROUP * L

    def finish_rows(row0):
        hb = acc_ref[0, pl.ds(row0, rows), :] + acc_ref[1, pl.ds(row0, rows), :]
        og = og_ref[pl.ds(row0, rows), :]
        yield
        gate = jax.nn.sigmoid(og.astype(F32))
        for h in range(ML_HEADS):
            sl = slice(h * ML_V, (h + 1) * ML_V)
            x = hb[:, sl]
            y = x * lax.rsqrt(jnp.mean(x * x, axis=-1, keepdims=True) + EPS) * hn[:, sl]
            y_ref[pl.ds(row0, rows), sl] = (y * gate[:, sl]).astype(y_ref.dtype)
            yield

    def group(j, carry, finish):
        (heads_f, m_f), (heads_b, m_b) = carry
        chunks = [(j * _ML_GROUP + k, nc - 1 - (j * _ML_GROUP + k)) for k in range(_ML_GROUP)]
        ms_f, ms_b = [m_f], [m_b]
        for cf, cb in chunks:
            ms_f.append(jnp.maximum(st_ref[cf][0:1, :] + ms_f[-1], st_ref[cf][1:2, :]))
            ms_b.append(jnp.maximum(st_ref[cb][0:1, :] + ms_b[-1], st_ref[cb][1:2, :]))
        gens = ([chunk_pre(cf, 0, ms_f[k]) for k, (cf, _) in enumerate(chunks)]
                + [chunk_pre(cb, 1, ms_b[k]) for k, (_, cb) in enumerate(chunks)])
        if finish:
            gens.append(finish_rows(pl.multiple_of((nc - _ML_GROUP * j) * L, rows)))
            gens.append(finish_rows(pl.multiple_of(_ML_GROUP * (j - 1) * L, rows)))
        pres = _run_interleaved(gens)
        for k in range(_ML_GROUP):
            heads_f = chunk_post(pres[k], heads_f)
            heads_b = chunk_post(pres[_ML_GROUP + k], heads_b)
        return (heads_f, ms_f[-1]), (heads_b, ms_b[-1])

    zero_state = jnp.zeros((ML_QK, ML_V), F32)
    init = (tuple((zero_state, zero_state) for _ in range(ML_HEADS)), jnp.zeros((1, LANES), F32))
    n_groups = nc // _ML_GROUP
    crossed = n_groups // 2 + 1
    carry = lax.fori_loop(0, crossed, functools.partial(group, finish=False), (init, init))
    lax.fori_loop(crossed, n_groups, functools.partial(group, finish=True), carry)
    _run_interleaved([finish_rows(0), finish_rows(t - rows)])


def _mlstm(qk, v, og, g_col, g_row, hn):
    b, t, _ = qk.shape
    nc = t // ML_CHUNK
    seq = lambda i: (i, 0, 0)
    return pl.pallas_call(
        _mlstm_kernel,
        out_shape=jax.ShapeDtypeStruct((b, t, ML_V_W), BF),
        grid=(b,),
        in_specs=[
            pl.BlockSpec((None, t, 2 * ML_QK_W), seq),
            pl.BlockSpec((None, t, ML_V_W), seq),
            pl.BlockSpec((None, t, ML_V_W), seq),
            pl.BlockSpec((None, t, LANES), seq),
            pl.BlockSpec((None, nc, PAIR_ROWS, LANES), lambda i: (i, 0, 0, 0)),
            pl.BlockSpec((1, ML_V_W), lambda i: (0, 0)),
        ],
        out_specs=pl.BlockSpec((None, t, ML_V_W), seq),
        scratch_shapes=[
            pltpu.VMEM((2, t, ML_V_W), F32),
            pltpu.VMEM((t, LANES), F32),
            pltpu.VMEM((t, LANES), F32),
            pltpu.VMEM((t, 2 * ML_QK_W), BF),
            pltpu.VMEM((nc, 8, LANES), F32),
        ],
        compiler_params=pltpu.CompilerParams(
            dimension_semantics=("parallel",), vmem_limit_bytes=VMEM_LIMIT),
    )(qk, v, og, g_col, g_row, hn)


_GD_QKV_W = 2 * GD_K_W + GD_V_W
_GD_PAIRS = GD_HEADS // 2
_GD_PD = 2 * _GD_PAIRS
_GD_ROW_A = 8
_GD_A_CHUNKS = 4


def _gdn_kernel(qkv_ref, z_ref, gc_ref, gr_ref, acol_ref, arow_ref, hn_ref, y_ref,
                acc_ref, u_ref, wq_ref, att_ref, kd_ref, egl_ref):
    t = qkv_ref.shape[0]
    L = GD_CHUNK
    nc = t // L
    rows = 256
    incl_lo, incl_up, _, _ = _tri_masks(L)
    lower_b = jnp.where(incl_lo, 1.0, 0.0).astype(BF)
    upper_b = jnp.where(incl_up, 1.0, 0.0).astype(BF)

    tt = lax.broadcasted_iota(jnp.int32, (L, LANES), 0)
    ln = lax.broadcasted_iota(jnp.int32, (L, LANES), 1)
    ss = ln % GD_K
    lane_lo = ln < GD_K
    incl = (ss <= tt, ss >= tt)
    strict = (ss < tt, ss > tt)
    eye_pair = jnp.where(ss == tt, 1.0, 0.0)
    bw_cols = (ln // GD_HEADS) % 2 == 1
    r2 = lax.broadcasted_iota(jnp.int32, (2 * L, LANES), 0)
    c2 = lax.broadcasted_iota(jnp.int32, (2 * L, LANES), 1)
    same_half = (r2 // L) == (c2 // GD_K)
    bd_upper = jnp.where(same_half & (c2 % GD_K >= r2 % L), 1.0, 0.0).astype(BF)
    bd_lower = jnp.where(same_half & (c2 % GD_K <= r2 % L), 1.0, 0.0).astype(BF)
    ej = lax.broadcasted_iota(jnp.int32, (LANES, _GD_PD * LANES), 0)
    en = lax.broadcasted_iota(jnp.int32, (LANES, _GD_PD * LANES), 1) // GD_K
    e_a = jnp.where(ej == GATE_A + en, 1.0, 0.0).astype(BF)
    e_b = jnp.where(ej == GATE_B + en, 1.0, 0.0).astype(BF)
    a_col = jnp.exp(acol_ref[...])
    a_row = jnp.exp(arow_ref[...])

    def phase_a(chunk_ids):
        chains = []
        for cidx in chunk_ids:
            off = pl.multiple_of(cidx * L, L)
            gcol = gc_ref[pl.ds(off, L), :]
            grow = gr_ref[cidx]
            qkv = qkv_ref[pl.ds(off, L), :]
            g_c = -a_col * _softplus(gcol)
            cum_c = jnp.where(bw_cols, _sel_dot_rhs(upper_b, g_c), _sel_dot_rhs(lower_b, g_c))
            gcb_all = _sel_dot_lhs(cum_c, e_a, parts=2)
            beb_all = _sel_dot_lhs(jax.nn.sigmoid(gcol), e_b, parts=1)
            g_r = -a_row * _softplus(grow)
            cum_r = (_sel_dot_lhs(g_r, bd_upper), _sel_dot_lhs(g_r, bd_lower))
            for p in range(_GD_PAIRS):
                q_b = qkv[:, p * LANES:(p + 1) * LANES]
                k_b = qkv[:, GD_K_W + p * LANES:GD_K_W + (p + 1) * LANES]
                q_p = q_b.astype(F32)
                k_p = k_b.astype(F32)
                v_p = qkv[:, 2 * GD_K_W + p * LANES:2 * GD_K_W + (p + 1) * LANES].astype(F32)
                kexp = _expand_pair(k_b, lane_lo)
                kk = _dot_nt(k_b, kexp)
                qk = _dot_nt(q_b, kexp)
                for d in range(2):
                    pd = d * _GD_PAIRS + p
                    gcb = gcb_all[:, pd * LANES:(pd + 1) * LANES]
                    beb = beb_all[:, pd * LANES:(pd + 1) * LANES]
                    gcr = cum_r[d][_GD_ROW_A + pd:_GD_ROW_A + pd + 1, :]
                    decay = jnp.exp(jnp.where(incl[d], gcb - gcr, -jnp.inf))
                    lmat = jnp.where(strict[d], beb * kk * decay, 0.0)
                    eg = jnp.exp(gcb)
                    gl = gcb[L - 1:L, :] if d == 0 else gcb[0:1, :]
                    x = jnp.concatenate([v_p * beb, k_p * (beb * eg)], axis=1)
                    att_ref[pd, pl.ds(off, L), :] = (qk * decay).astype(BF)
                    wq_ref[pd, cidx, L:2 * L, :] = (q_p * eg).astype(BF)
                    kd_ref[pd, pl.ds(off, L), :] = (k_p * jnp.exp(gl - gcb)).astype(BF)
                    egl_ref[cidx, pd] = jnp.broadcast_to(jnp.exp(gl), (8, LANES))
                    chains.append(dict(pd=pd, cidx=cidx, off=off, p=lmat.astype(BF), l=lmat, x=x))
            yield

        for ch in chains:
            ch["t"] = eye_pair - ch["l"]
            ch["p"] = _dot(ch["p"], _expand_pair(ch["p"], lane_lo)).astype(BF)
        yield
        power = 2
        while power < L:
            last = 2 * power >= L
            for ch in chains:
                rhs = _expand_pair(ch["t"].astype(BF), lane_lo)
                if not last:
                    rhs = jnp.concatenate([_expand_pair(ch["p"], lane_lo), rhs], axis=1)
                ch["y"] = _dot(ch["p"], rhs)
            yield
            for ch in chains:
                if not last:
                    ch["p"] = ch["y"][:, :LANES].astype(BF)
                ch["t"] = ch["t"] + ch["y"][:, -LANES:]
            power *= 2
        for ch in chains:
            xb = ch["x"].astype(BF)
            rhs = jnp.concatenate([_expand_pair(xb[:, :LANES], lane_lo),
                                   _expand_pair(xb[:, LANES:], lane_lo)], axis=1)
            sol = _dot(ch["t"].astype(BF), rhs)
            u_ref[ch["pd"], pl.ds(ch["off"], L), :] = sol[:, :LANES]
            wq_ref[ch["pd"], ch["cidx"], 0:L, :] = sol[:, LANES:].astype(BF)

    rb = lax.broadcasted_iota(jnp.int32, (LANES, LANES), 0) // GD_K
    cb = lax.broadcasted_iota(jnp.int32, (LANES, LANES), 1) // GD_V
    blockdiag = rb == cb

    def phase_b(steps, states):
        loaded = []
        for i in steps:
            items = []
            for d in range(2):
                cidx = i if d == 0 else nc - 1 - i
                off = pl.multiple_of(cidx * L, L)
                for p in range(_GD_PAIRS):
                    pd = d * _GD_PAIRS + p
                    items.append(dict(
                        d=d, p=p, off=off, wq=wq_ref[pd, cidx], u=u_ref[pd, pl.ds(off, L), :],
                        kd=kd_ref[pd, pl.ds(off, L), :], att=att_ref[pd, pl.ds(off, L), :],
                        egl=egl_ref[cidx, pd][0:1, :]))
            loaded.append(items)
        yield
        for items in loaded:
            s_b = [s.astype(BF) for s in states]
            res = [_dot(it["wq"], s_b[pd]) for pd, it in enumerate(items)]
            yield
            v_new = [(it["u"] - res[pd][:L]).astype(BF) for pd, it in enumerate(items)]
            upd = [_dot_tn(it["kd"], v_new[pd]) for pd, it in enumerate(items)]
            yield
            states = tuple(states[pd] * it["egl"] + jnp.where(blockdiag, upd[pd], 0.0)
                           for pd, it in enumerate(items))
            outs = [res[pd][L:] + _dot(it["att"], _expand_pair(v_new[pd], lane_lo))
                    for pd, it in enumerate(items)]
            yield
            for it, o in zip(items, outs):
                acc_ref[it["d"], pl.ds(it["off"], L), it["p"] * LANES:(it["p"] + 1) * LANES] = o
        return states

    half = _GD_A_CHUNKS // 2
    n_groups = nc // _GD_A_CHUNKS

    def group_chunks(j):
        return [j * half + k for k in range(half)] + [nc - 1 - (j * half + k) for k in range(half)]

    def overlapped(j, states):
        steps = [(j - 1) * half + k for k in range(half)]
        return _run_interleaved([phase_b(steps, states), phase_a(group_chunks(j))])[0]

    hn = hn_ref[...]
    gsum_v = _group_ones(GD_V_W, GD_V)

    def finish_chunks(chunk_ids):
        loaded = []
        for c in chunk_ids:
            off = pl.multiple_of(c * L, L)
            loaded.append((off, acc_ref[0, pl.ds(off, L), :] + acc_ref[1, pl.ds(off, L), :],
                           z_ref[pl.ds(off, L), :]))
        yield
        for off, x, z in loaded:
            z = z.astype(F32)
            ms = _sel_dot_lhs(x * x, gsum_v, parts=2) * (1.0 / GD_V)
            yield
            y = x * lax.rsqrt(ms + EPS) * hn
            y_ref[pl.ds(off, L), :] = (y * (z * jax.nn.sigmoid(z))).astype(y_ref.dtype)
            yield

    def drain(i, states):
        return _run_interleaved([phase_b([i], states)])[0]

    def drain_finish(i, states):
        return _run_interleaved([phase_b([i], states), finish_chunks([i - 1, nc - i])])[0]

    first_finish = nc // 2 + 1
    _run_interleaved([phase_a(group_chunks(0))])
    states = tuple(jnp.zeros((LANES, LANES), F32) for _ in range(_GD_PD))
    states = lax.fori_loop(1, n_groups, overlapped, states)
    states = lax.fori_loop((n_groups - 1) * half, first_finish, drain, states, unroll=True)
    states = lax.fori_loop(first_finish, nc, drain_finish, states, unroll=3)
    _run_interleaved([finish_chunks([0, nc - 1])])


def _gdn(qkv, z, g_col, g_rowp, a_col, a_rowp, hn):
    b, t, _ = qkv.shape
    nc = t // GD_CHUNK
    seq = lambda i: (i, 0, 0)
    const = lambda i: (0, 0)
    return pl.pallas_call(
        _gdn_kernel,
        out_shape=jax.ShapeDtypeStruct((b, t, GD_V_W), BF),
        grid=(b,),
        in_specs=[
            pl.BlockSpec((None, t, _GD_QKV_W), seq),
            pl.BlockSpec((None, t, GD_V_W), seq),
            pl.BlockSpec((None, t, LANES), seq),
            pl.BlockSpec((None, nc, PAIR_ROWS, LANES), lambda i: (i, 0, 0, 0)),
            pl.BlockSpec((1, LANES), const),
            pl.BlockSpec((PAIR_ROWS, LANES), const),
            pl.BlockSpec((1, GD_V_W), const),
        ],
        out_specs=pl.BlockSpec((None, t, GD_V_W), seq),
        scratch_shapes=[
            pltpu.VMEM((2, t, GD_V_W), F32),
            pltpu.VMEM((_GD_PD, t, LANES), F32),
            pltpu.VMEM((_GD_PD, nc, 2 * GD_CHUNK, LANES), BF),
            pltpu.VMEM((_GD_PD, t, LANES), BF),
            pltpu.VMEM((_GD_PD, t, LANES), BF),
            pltpu.VMEM((nc, _GD_PD, 8, LANES), F32),
        ],
        compiler_params=pltpu.CompilerParams(
            dimension_semantics=("parallel",), vmem_limit_bytes=VMEM_LIMIT),
    )(qkv, z, g_col, g_rowp, a_col, a_rowp, hn)


def _outproj_kernel(x_ref, ya_ref, yb_ref, yc_ref, w_ref, o_ref):
    acc = x_ref[...]
    acc = acc + _dot(ya_ref[...], w_ref[0:GM_WIDTH, :])
    acc = acc + _dot(yb_ref[...], w_ref[GM_WIDTH:GM_WIDTH + ML_V_W, :])
    acc = acc + _dot(yc_ref[...], w_ref[GM_WIDTH + ML_V_W:, :])
    o_ref[...] = acc


def _outproj(x2, ya, yb, yc, w, layer, tm):
    n = x2.shape[0]
    row = lambda i: (i, 0)
    return pl.pallas_call(
        _outproj_kernel,
        out_shape=jax.ShapeDtypeStruct((n, D_MODEL), F32),
        grid=(n // tm,),
        in_specs=[
            pl.BlockSpec((tm, D_MODEL), row),
            pl.BlockSpec((tm, GM_WIDTH), row),
            pl.BlockSpec((tm, ML_V_W), row),
            pl.BlockSpec((tm, GD_V_W), row),
            _layer_spec((D_MODEL, D_MODEL), layer),
        ],
        out_specs=pl.BlockSpec((tm, D_MODEL), row),
        compiler_params=pltpu.CompilerParams(
            dimension_semantics=("parallel",), vmem_limit_bytes=VMEM_LIMIT),
    )(x2, ya, yb, yc, w)


_FFN_HALO = 16
_FFN_CHUNK = 256
_FFN_DOWN_GROUP = 4
_FFN_SLOTS = 3


def _ffn_kernel(xm_ref, xp_ref, xn_ref, g_ref, wup_ref, cw_ref, cb_ref, wdn_ref, gf_ref, o_ref,
                h_ref, act_ref, up_ref, *, tiles_per_seq, final_norm):
    i = pl.program_id(0)
    tm = xm_ref.shape[0]
    g = g_ref[...]
    pos = i % tiles_per_seq

    def norm(x):
        return x * lax.rsqrt(jnp.mean(x * x, axis=-1, keepdims=True) + EPS) * g

    hp = jnp.where(pos == 0, 0.0, norm(xp_ref[...]))
    hn = jnp.where(pos == tiles_per_seq - 1, 0.0, norm(xn_ref[...]))
    xm = xm_ref[...]
    h_ref[0:_FFN_HALO, :] = hp.astype(BF)
    h_ref[_FFN_HALO:_FFN_HALO + tm, :] = norm(xm).astype(BF)
    h_ref[_FFN_HALO + tm:, :] = hn.astype(BF)
    hh = h_ref[...]
    cw = cw_ref[...]
    cb = cb_ref[...]
    o_ref[...] = xm

    def up_project(j):
        c0 = j * _FFN_CHUNK
        slot = j % _FFN_SLOTS
        up_ref[slot, :, 0:_FFN_CHUNK] = _dot(hh, wup_ref[:, c0:c0 + _FFN_CHUNK])
        up_ref[slot, :, _FFN_CHUNK:] = _dot(hh, wup_ref[:, D_FF + c0:D_FF + c0 + _FFN_CHUNK])

    def conv(slot, lanes, c0):
        sl = slice(c0, c0 + _FFN_CHUNK)
        return (cw[0:1, sl] * up_ref[slot, _FFN_HALO - 1:_FFN_HALO - 1 + tm, lanes]
                + cw[1:2, sl] * up_ref[slot, _FFN_HALO:_FFN_HALO + tm, lanes]
                + cw[2:3, sl] * up_ref[slot, _FFN_HALO + 1:_FFN_HALO + 1 + tm, lanes] + cb[:, sl])

    n_chunks = D_FF // _FFN_CHUNK
    for j in range(_FFN_SLOTS - 1):
        up_project(j)
    for j in range(n_chunks):
        if j + _FFN_SLOTS - 1 < n_chunks:
            up_project(j + _FFN_SLOTS - 1)
        c0 = j * _FFN_CHUNK
        gt = conv(j % _FFN_SLOTS, slice(0, _FFN_CHUNK), c0)
        vl = conv(j % _FFN_SLOTS, slice(_FFN_CHUNK, 2 * _FFN_CHUNK), D_FF + c0)
        act_ref[:, c0:c0 + _FFN_CHUNK] = (gt * jax.nn.sigmoid(gt) * vl).astype(BF)
        if (j + 1) % _FFN_DOWN_GROUP == 0 or j + 1 == n_chunks:
            k0 = (j // _FFN_DOWN_GROUP) * _FFN_DOWN_GROUP * _FFN_CHUNK
            k1 = c0 + _FFN_CHUNK
            o_ref[...] += _dot(act_ref[:, k0:k1], wdn_ref[k0:k1, :])

    if final_norm:
        y = o_ref[...]
        o_ref[...] = y * lax.rsqrt(jnp.mean(y * y, axis=-1, keepdims=True) + EPS) * gf_ref[...]


def _ffn(x2, gain, w_up, conv_w, conv_b, w_down, gain_final, layer, seq_len, tm, final_norm):
    n = x2.shape[0]
    hb = tm // _FFN_HALO
    nhb = n // _FFN_HALO
    row = lambda i: (i, 0)
    const = lambda i: (0, 0)
    kern = functools.partial(_ffn_kernel, tiles_per_seq=seq_len // tm, final_norm=final_norm)
    return pl.pallas_call(
        kern,
        out_shape=jax.ShapeDtypeStruct((n, D_MODEL), F32),
        grid=(n // tm,),
        in_specs=[
            pl.BlockSpec((tm, D_MODEL), row),
            pl.BlockSpec((_FFN_HALO, D_MODEL), lambda i: (jnp.maximum(i * hb - 1, 0), 0)),
            pl.BlockSpec((_FFN_HALO, D_MODEL), lambda i: (jnp.minimum((i + 1) * hb, nhb - 1), 0)),
            pl.BlockSpec((1, D_MODEL), const),
            _layer_spec((D_MODEL, 2 * D_FF), layer, pipeline_mode=pl.Buffered(1)),
            pl.BlockSpec((FFN_CONV, 2 * D_FF), const),
            pl.BlockSpec((1, 2 * D_FF), const),
            _layer_spec((D_FF, D_MODEL), layer, pipeline_mode=pl.Buffered(1)),
            pl.BlockSpec((1, D_MODEL), const),
        ],
        out_specs=pl.BlockSpec((tm, D_MODEL), row),
        scratch_shapes=[
            pltpu.VMEM((tm + 2 * _FFN_HALO, D_MODEL), BF),
            pltpu.VMEM((tm, D_FF), BF),
            pltpu.VMEM((_FFN_SLOTS, tm + 2 * _FFN_HALO, 2 * _FFN_CHUNK), F32),
        ],
        compiler_params=pltpu.CompilerParams(
            dimension_semantics=("parallel",), vmem_limit_bytes=VMEM_LIMIT),
    )(x2, x2, x2, gain, w_up, conv_w, conv_b, w_down, gain_final)


_TM = 512
_TM_FFN = 1024


def _stacked_matmul_weights(w_in, w_out, w_up, w_down):
    depth = w_in.shape[0]
    g0 = 2 * GM_WIDTH + 2 * ML_QK_W + 2 * ML_V_W
    g1 = g0 + 4 * ML_HEADS
    c1 = g1 + _GD_QKV_W + GD_V_W
    q0 = 2 * GM_WIDTH
    col_scale = jnp.ones((_IN_MAIN,), F32).at[q0:q0 + ML_QK_W].set(ML_QK ** -0.5)
    w_main = (jnp.concatenate([w_in[:, :, :g0], w_in[:, :, g1:c1]], axis=2) * col_scale).astype(BF)
    w_gate = jnp.concatenate(
        [w_in[:, :, g0:g1], w_in[:, :, c1:], jnp.zeros((depth, D_MODEL, LANES - N_GATES), F32)],
        axis=2).astype(BF)
    return w_main, w_gate, w_out.astype(BF), w_up.astype(BF), w_down.astype(BF)


def _layer(x2, batch, seq_len, layer, big, norm_mix, gm_norm, gm_ws, gm_bs, ml_gate_bias, ml_head_norm,
           gd_conv, gd_a_log, gd_dt_bias, gd_head_norm, norm_ffn, ffn_conv, ffn_conv_b,
           norm_final, final_norm):
    n = x2.shape[0]
    nc = seq_len // GD_CHUNK
    w_main, w_gate, w_out, w_up, w_down = big
    gate_bias = jnp.concatenate(
        [ml_gate_bias, gd_dt_bias.reshape(-1), jnp.zeros((LANES - GATE_B,), F32)]).reshape(1, LANES)
    uv, qk, v, og, cq, z, gates = _inproj(
        x2, norm_mix.reshape(1, D_MODEL), w_main, w_gate, gate_bias, gd_conv, layer, seq_len, _TM)

    seq = lambda a: a.reshape(batch, seq_len, a.shape[-1])
    g_col = seq(gates)
    g32 = g_col[:, :, :N_GATES]
    g_rowp = g32.reshape(batch, nc, GD_CHUNK, 4, 2, _GD_PAIRS, 2).transpose(0, 1, 3, 4, 5, 6, 2)
    g_rowp = g_rowp.reshape(batch, nc, PAIR_ROWS, LANES)

    ws_cat = gm_ws.transpose(1, 0, 2).reshape(GM_CHUNK, GM_GROUPS * GM_CHUNK).astype(BF)
    bs_exp = jnp.repeat(gm_bs.T, GM_DIM, axis=1)
    ya = _gmlp(seq(uv), gm_norm.reshape(1, GM_WIDTH), ws_cat, bs_exp)

    yb = _mlstm(seq(qk), seq(v), seq(og), g_col, g_rowp, ml_head_norm.reshape(1, ML_V_W))

    a_flat = gd_a_log.reshape(-1)
    a_col = jnp.zeros((LANES,), F32).at[GATE_A:GATE_B].set(a_flat).reshape(1, LANES)
    a_rowp = jnp.zeros((PAIR_ROWS, LANES), F32).at[_GD_ROW_A:_GD_ROW_A + _GD_PD].set(
        jnp.repeat(a_flat.reshape(_GD_PD, 2), GD_K, axis=1))
    yc = _gdn(seq(cq), seq(z), g_col, g_rowp, a_col, a_rowp, gd_head_norm.reshape(1, GD_V_W))

    flat = lambda a: a.reshape(n, a.shape[-1])
    x2 = _outproj(x2, flat(ya), flat(yb), flat(yc), w_out, layer, _TM)
    return _ffn(x2, norm_ffn.reshape(1, D_MODEL), w_up, ffn_conv, ffn_conv_b.reshape(1, -1),
                w_down, norm_final.reshape(1, D_MODEL), layer, seq_len, _TM_FFN, final_norm)


def kernel(x, norm_mix, w_in, gm_norm, gm_ws, gm_bs, ml_gate_bias, ml_head_norm, gd_conv, gd_A_log,
           gd_dt_bias, gd_head_norm, w_out, norm_ffn, w_up, ffn_conv, ffn_conv_b, w_down, norm_final):
    batch, seq_len, _ = x.shape
    depth = w_in.shape[0]
    x2 = x.reshape(batch * seq_len, D_MODEL)
    big = _stacked_matmul_weights(w_in, w_out, w_up, w_down)
    for i in range(depth):
        x2 = _layer(x2, batch, seq_len, i, big, norm_mix[i], gm_norm[i], gm_ws[i], gm_bs[i],
                    ml_gate_bias[i], ml_head_norm[i], gd_conv[i], gd_A_log[i], gd_dt_bias[i],
                    gd_head_norm[i], norm_ffn[i], ffn_conv[i], ffn_conv_b[i],
                    norm_final, i == depth - 1)
    return x2.reshape(batch, seq_len, D_MODEL)
```

```python
import functools

import jax
import jax.numpy as jnp
from jax import lax
from jax.experimental import pallas as pl
from jax.experimental.pallas import tpu as pltpu

D_MODEL = 1024
GM_GROUPS = 4
GM_DIM = 64
GM_WIDTH = GM_GROUPS * GM_DIM
GM_CHUNK = 128
ML_HEADS = 4
ML_QK = 64
ML_V = 128
ML_QK_W = ML_HEADS * ML_QK
ML_V_W = ML_HEADS * ML_V
ML_CHUNK = 64
GD_HEADS = 4
GD_K = 64
GD_V = 64
GD_K_W = GD_HEADS * GD_K
GD_V_W = GD_HEADS * GD_V
GD_CHUNK = 64
GD_CONV = 5
D_FF = 2816
FFN_CONV = 3
EPS = 1e-6

LANES = 128
N_GATES = 32
GATE_I, GATE_F, GATE_A, GATE_B = 0, 8, 16, 24
PAIR_ROWS = 16
VMEM_LIMIT = 56 * 1024 * 1024

BF = jnp.bfloat16
F32 = jnp.float32


def _dot(a, b):
    return jnp.dot(a, b, preferred_element_type=F32)


def _dot_nt(a, b):
    return lax.dot_general(a, b, (((1,), (1,)), ((), ())), preferred_element_type=F32)


def _dot_tn(a, b):
    return lax.dot_general(a, b, (((0,), (0,)), ((), ())), preferred_element_type=F32)


def _split(x, parts):
    out = []
    for _ in range(parts - 1):
        hi = x.astype(BF)
        out.append(hi)
        x = x - hi.astype(F32)
    out.append(x.astype(BF))
    return out


def _sel_dot_rhs(sel, x, parts=3):
    acc = None
    for term in _split(x, parts):
        y = _dot(sel, term)
        acc = y if acc is None else acc + y
    return acc


def _sel_dot_lhs(x, sel, parts=3):
    acc = None
    for term in _split(x, parts):
        y = _dot(term, sel)
        acc = y if acc is None else acc + y
    return acc


def _expand_pair(xb, lane_lo):
    z = jnp.zeros((), xb.dtype)
    return jnp.concatenate([jnp.where(lane_lo, xb, z), jnp.where(lane_lo, z, xb)], axis=0)


def _run_interleaved(gens):
    results = [None] * len(gens)
    pending = list(range(len(gens)))
    while pending:
        for k in list(pending):
            try:
                next(gens[k])
            except StopIteration as stop:
                results[k] = stop.value
                pending.remove(k)
    return tuple(results)


def _running_max(x, reverse):
    n, w = x.shape
    sh = 1
    while sh < n:
        fill = jnp.full((sh, w), -jnp.inf, x.dtype)
        if reverse:
            shifted = jnp.concatenate([x[sh:], fill], axis=0)
        else:
            shifted = jnp.concatenate([fill, x[:n - sh]], axis=0)
        x = jnp.maximum(x, shifted)
        sh *= 2
    return x


def _pair_cumsum_mats():
    r = lax.broadcasted_iota(jnp.int32, (LANES, LANES), 0)
    c = lax.broadcasted_iota(jnp.int32, (LANES, LANES), 1)
    half = LANES // 2
    same = (r // half) == (c // half)
    prefix = jnp.where(same & (c >= r), 1.0, 0.0).astype(BF)
    suffix = jnp.where(same & (c <= r), 1.0, 0.0).astype(BF)
    return prefix, suffix


def _column_spread(first_col, n_cols, width):
    j = lax.broadcasted_iota(jnp.int32, (LANES, n_cols * width), 0)
    n = lax.broadcasted_iota(jnp.int32, (LANES, n_cols * width), 1) // width
    return jnp.where(j == first_col + n, 1.0, 0.0).astype(BF)


def _softplus(x):
    return jnp.maximum(x, 0.0) + jnp.log1p(jnp.exp(-jnp.abs(x)))


def _log_sigmoid(x):
    return jnp.minimum(x, 0.0) - jnp.log1p(jnp.exp(-jnp.abs(x)))


def _tri_masks(n):
    r = lax.broadcasted_iota(jnp.int32, (n, n), 0)
    c = lax.broadcasted_iota(jnp.int32, (n, n), 1)
    return c <= r, c >= r, c < r, c > r


def _group_ones(width, group):
    r = lax.broadcasted_iota(jnp.int32, (width, width), 0) // group
    c = lax.broadcasted_iota(jnp.int32, (width, width), 1) // group
    return jnp.where(r == c, 1.0, 0.0).astype(BF)


_IN_SEGS = ((0, 512), (512, 1024), (1024, 1536), (1536, 2048), (2048, 2816), (2816, 3072))
_IN_MAIN = 3072


_IN_HALO = 16
_IN_CONV_SEG = 4


def _inproj_kernel(x_ref, xp_ref, xn_ref, g_ref, w_ref, wg_ref, b_ref, cw_ref,
                   o_uv, o_qk, o_v, o_o, o_c, o_z, o_g, h_ref, c_ref, *, tiles_per_seq):
    tm = x_ref.shape[0]
    g = g_ref[...]
    pos = pl.program_id(0) % tiles_per_seq

    def norm(x):
        return x * lax.rsqrt(jnp.mean(x * x, axis=-1, keepdims=True) + EPS) * g

    h_ref[0:_IN_HALO, :] = jnp.where(pos == 0, 0.0, norm(xp_ref[...])).astype(BF)
    h_ref[_IN_HALO:_IN_HALO + tm, :] = norm(x_ref[...]).astype(BF)
    h_ref[_IN_HALO + tm:, :] = jnp.where(pos == tiles_per_seq - 1, 0.0, norm(xn_ref[...])).astype(BF)
    lo, hi = _IN_SEGS[_IN_CONV_SEG]
    c_ref[...] = _dot(h_ref[...], w_ref[:, lo:hi])
    h = h_ref[_IN_HALO:_IN_HALO + tm, :]
    for idx, o in enumerate((o_uv, o_qk, o_v, o_o, o_c, o_z)):
        if idx != _IN_CONV_SEG:
            lo, hi = _IN_SEGS[idx]
            o[...] = _dot(h, w_ref[:, lo:hi]).astype(o.dtype)
    o_g[...] = _dot(h, wg_ref[...]) + b_ref[...]

    cw = cw_ref[...]
    gsum = _group_ones(LANES, GD_K)
    pad = (GD_CONV - 1) // 2
    for s in range(_GD_QKV_W // LANES):
        sl = slice(s * LANES, (s + 1) * LANES)
        y = jnp.zeros((tm, LANES), F32)
        for tap in range(GD_CONV):
            r0 = _IN_HALO + tap - pad
            y = y + cw[tap:tap + 1, sl] * c_ref[r0:r0 + tm, sl]
        y = y * jax.nn.sigmoid(y)
        if s < 2 * GD_K_W // LANES:
            scale = GD_K ** -0.5 if s < GD_K_W // LANES else 1.0
            y = y * (lax.rsqrt(_sel_dot_lhs(y * y, gsum, parts=2) + EPS) * scale)
        o_c[:, sl] = y.astype(o_c.dtype)


def _layer_spec(shape, layer, **kw):
    zeros = (0,) * len(shape)
    return pl.BlockSpec((None,) + tuple(shape), lambda i: (layer,) + zeros, **kw)


def _inproj(x2, gain, w_main, w_gate, gate_bias, conv_w, layer, seq_len, tm):
    n = x2.shape[0]
    widths = [hi - lo for lo, hi in _IN_SEGS]
    out_shape = [jax.ShapeDtypeStruct((n, w), BF) for w in widths]
    out_shape.append(jax.ShapeDtypeStruct((n, LANES), F32))
    hb = tm // _IN_HALO
    nhb = n // _IN_HALO
    row = lambda i: (i, 0)
    const = lambda i: (0, 0)
    return pl.pallas_call(
        functools.partial(_inproj_kernel, tiles_per_seq=seq_len // tm),
        out_shape=out_shape,
        grid=(n // tm,),
        in_specs=[
            pl.BlockSpec((tm, D_MODEL), row),
            pl.BlockSpec((_IN_HALO, D_MODEL), lambda i: (jnp.maximum(i * hb - 1, 0), 0)),
            pl.BlockSpec((_IN_HALO, D_MODEL), lambda i: (jnp.minimum((i + 1) * hb, nhb - 1), 0)),
            pl.BlockSpec((1, D_MODEL), const),
            _layer_spec((D_MODEL, _IN_MAIN), layer),
            _layer_spec((D_MODEL, LANES), layer),
            pl.BlockSpec((1, LANES), const),
            pl.BlockSpec((GD_CONV, _GD_QKV_W), const),
        ],
        out_specs=[pl.BlockSpec((tm, w), row) for w in widths] + [pl.BlockSpec((tm, LANES), row)],
        scratch_shapes=[
            pltpu.VMEM((tm + 2 * _IN_HALO, D_MODEL), BF),
            pltpu.VMEM((tm + 2 * _IN_HALO, _GD_QKV_W), F32),
        ],
        compiler_params=pltpu.CompilerParams(
            dimension_semantics=("parallel",), vmem_limit_bytes=VMEM_LIMIT),
    )(x2, x2, x2, gain, w_main, w_gate, gate_bias, conv_w)


def _gmlp_kernel(uv_ref, gn_ref, ws_ref, bs_ref, y_ref):
    t = uv_ref.shape[0]
    gsum = _group_ones(GM_WIDTH, GM_DIM)
    rr = lax.broadcasted_iota(jnp.int32, (GM_GROUPS * GM_CHUNK, GM_WIDTH), 0) // GM_CHUNK
    cc = lax.broadcasted_iota(jnp.int32, (GM_GROUPS * GM_CHUNK, GM_WIDTH), 1) // GM_DIM
    blockdiag = rr == cc
    ws = ws_ref[...]
    bs = bs_ref[...]
    gn = gn_ref[...]

    def body(c, carry):
        off = pl.multiple_of(c * GM_CHUNK, GM_CHUNK)
        uv = uv_ref[pl.ds(off, GM_CHUNK), :].astype(F32)
        u = jax.nn.gelu(uv[:, :GM_WIDTH])
        v = jax.nn.gelu(uv[:, GM_WIDTH:])
        ms = _sel_dot_lhs(v * v, gsum, parts=2) * (1.0 / GM_DIM)
        vn = (v * lax.rsqrt(ms + EPS) * gn).astype(BF)
        vexp = jnp.where(blockdiag, jnp.concatenate([vn] * GM_GROUPS, axis=0), jnp.zeros((), BF))
        sg = _dot(ws, vexp) + bs
        y_ref[pl.ds(off, GM_CHUNK), :] = (u * sg).astype(y_ref.dtype)
        return carry

    lax.fori_loop(0, t // GM_CHUNK, body, 0, unroll=4)


def _gmlp(uv, gn, ws_cat, bs_exp):
    b, t, _ = uv.shape
    const = lambda i: (0, 0)
    return pl.pallas_call(
        _gmlp_kernel,
        out_shape=jax.ShapeDtypeStruct((b, t, GM_WIDTH), BF),
        grid=(b,),
        in_specs=[
            pl.BlockSpec((None, t, 2 * GM_WIDTH), lambda i: (i, 0, 0)),
            pl.BlockSpec((1, GM_WIDTH), const),
            pl.BlockSpec((GM_CHUNK, GM_GROUPS * GM_CHUNK), const),
            pl.BlockSpec((GM_CHUNK, GM_WIDTH), const),
        ],
        out_specs=pl.BlockSpec((None, t, GM_WIDTH), lambda i: (i, 0, 0)),
        compiler_params=pltpu.CompilerParams(
            dimension_semantics=("parallel",), vmem_limit_bytes=VMEM_LIMIT),
    )(uv, gn, ws_cat, bs_exp)


_ML_GROUP = 4


def _mlstm_kernel(qk_ref, v_ref, og_ref, gc_ref, gr_ref, hn_ref, y_ref,
                  acc_ref, bc_ref, cm_ref, kw_ref, st_ref):
    t = qk_ref.shape[0]
    L = ML_CHUNK
    nc = t // L
    lower, upper, _, _ = _tri_masks(L)
    lower_b = jnp.where(lower, 1.0, 0.0).astype(BF)
    upper_b = jnp.where(upper, 1.0, 0.0).astype(BF)
    ones_v = jnp.ones((L, ML_V), BF)
    zero_v = jnp.zeros((L, ML_V), BF)
    zero_c = jnp.zeros((ML_QK, ML_V), BF)
    tt = lax.broadcasted_iota(jnp.int32, (L, LANES), 0)
    ln = lax.broadcasted_iota(jnp.int32, (L, LANES), 1)
    lane_lo = ln < ML_QK
    mask = (ln % ML_QK <= tt, ln % ML_QK >= tt)
    f_cols = (ln >= GATE_F) & (ln < GATE_F + 2 * ML_HEADS)
    pair_prefix, pair_suffix = _pair_cumsum_mats()
    ro = lax.broadcasted_iota(jnp.int32, (LANES, 2 * ML_V), 0) // ML_QK
    co = lax.broadcasted_iota(jnp.int32, (LANES, 2 * ML_V), 1) // ML_V
    pair_rowsum = jnp.where(ro == co, 1.0, 0.0).astype(BF)
    bw_cols = (ln // ML_HEADS) % 2 == 1
    spread_k_all = _column_spread(GATE_F, 2 * ML_HEADS, ML_QK)

    def gate_stats(c):
        off = pl.multiple_of(c * L, L)
        gcol = gc_ref[pl.ds(off, L), :]
        grow = gr_ref[c]
        lf = _log_sigmoid(gcol)
        lf_r = _log_sigmoid(grow)
        bc = jnp.where(bw_cols, _sel_dot_rhs(upper_b, lf), _sel_dot_rhs(lower_b, lf))
        cum_f = _sel_dot_lhs(lf_r, pair_prefix)
        cum_b = _sel_dot_lhs(lf_r, pair_suffix)
        yield
        ig = pltpu.roll(gcol, GATE_F - GATE_I, 1)
        b_last = jnp.where(bw_cols[0:1], bc[0:1, :], bc[L - 1:L, :])
        a = b_last - bc + ig
        m_loc = jnp.max(a, axis=0, keepdims=True)
        wa_k = _sel_dot_lhs(jnp.where(f_cols, jnp.exp(a - m_loc), 0.0), spread_k_all, parts=1)
        yield
        r = ig - bc
        bc_ref[pl.ds(off, L), :] = bc
        cm_ref[pl.ds(off, L), :] = jnp.where(bw_cols, _running_max(r, True), _running_max(r, False))
        hp = ML_HEADS // 2
        rr = jnp.concatenate([grow[0:hp] - cum_f[ML_HEADS:ML_HEADS + hp],
                              grow[hp:2 * hp] - cum_b[ML_HEADS + hp:ML_HEADS + 2 * hp]], axis=0)
        st_ref[c] = jnp.concatenate([b_last, m_loc, rr, jnp.zeros((8 - 2 - 2 * hp, LANES), F32)], axis=0)
        yield
        for p in range(hp):
            k_p = qk_ref[pl.ds(off, L), ML_QK_W + p * LANES:ML_QK_W + (p + 1) * LANES].astype(F32)
            for d in range(2):
                lanes = slice((2 * d + p) * LANES, (2 * d + p + 1) * LANES)
                kw_ref[pl.ds(off, L), lanes] = (k_p * wa_k[:, lanes]).astype(BF)

    def gate_stats_group(j, carry):
        _run_interleaved([gate_stats(j * _ML_GROUP + k) for k in range(_ML_GROUP)])
        return carry

    lax.fori_loop(0, nc // _ML_GROUP, gate_stats_group, 0)

    def chunk_pre(cidx, d, m):
        off = pl.multiple_of(cidx * L, L)
        qkc = qk_ref[pl.ds(off, L), :]
        vc = v_ref[pl.ds(off, L), :]
        col0 = GATE_F + ML_HEADS * d
        spread_k = _column_spread(col0, ML_HEADS, ML_QK)
        spread_v = _column_spread(col0, ML_HEADS, ML_V)
        bc = bc_ref[pl.ds(off, L), :]
        stats = st_ref[cidx]
        b_last = stats[0:1, :]
        m_loc = stats[1:2, :]
        mx = jnp.maximum(m, cm_ref[pl.ds(off, L), :])
        m_new = jnp.maximum(b_last + m, m_loc)
        f_rows = jnp.concatenate(
            [jnp.exp(b_last + m - m_new), jnp.exp(m_loc - m_new), jnp.zeros((6, LANES), F32)], axis=0)
        clean = lambda x: jnp.where(f_cols[:x.shape[0]], x, 0.0)
        mx_k = _sel_dot_lhs(clean(mx), spread_k, parts=2)
        f_v = _sel_dot_lhs(clean(f_rows), spread_v)
        w_col = jnp.exp(m - mx)
        e_col = jnp.exp(-(bc + mx))

        def head_lanes(x, p):
            cols = [jnp.broadcast_to(x[:, col0 + 2 * p + hh:col0 + 2 * p + hh + 1], (L, ML_V)) for hh in range(2)]
            return jnp.concatenate(cols, axis=1)

        yield
        pairs = []
        for p in range(ML_HEADS // 2):
            row = 2 * d + p
            q_p = qkc[:, p * LANES:(p + 1) * LANES]
            k_p = qkc[:, ML_QK_W + p * LANES:ML_QK_W + (p + 1) * LANES]
            v0 = vc[:, 2 * p * ML_V:(2 * p + 1) * ML_V]
            v1 = vc[:, (2 * p + 1) * ML_V:(2 * p + 2) * ML_V]
            qk = _dot_nt(q_p, _expand_pair(k_p, lane_lo))
            rr = stats[2 + row:3 + row, :]
            s = qk * jnp.exp(jnp.where(mask[d], rr - mx_k[:, p * LANES:(p + 1) * LANES], -jnp.inf))
            rhs_s = jnp.concatenate(
                [jnp.concatenate([v0, zero_v], axis=1), jnp.concatenate([zero_v, v1], axis=1)], axis=0)
            intra = _dot(s.astype(BF), jnp.concatenate([rhs_s, pair_rowsum], axis=1))
            yield
            kw = kw_ref[pl.ds(off, L), row * LANES:(row + 1) * LANES]
            upd = _dot_tn(kw, jnp.concatenate([v0, v1, ones_v], axis=1))
            pairs.append(dict(q=q_p, intra=intra, upd=upd,
                              w2=head_lanes(w_col, p), e2=head_lanes(e_col, p)))
            yield
        return dict(d=d, off=off, pairs=pairs, f_v=f_v)

    def chunk_post(pre, heads):
        new_heads = []
        for p, pp in enumerate(pre["pairs"]):
            (c0, n0), (c1, n1) = heads[2 * p], heads[2 * p + 1]
            rhs_c = jnp.concatenate(
                [jnp.concatenate([c0.astype(BF), zero_c, n0.astype(BF), zero_c], axis=1),
                 jnp.concatenate([zero_c, c1.astype(BF), zero_c, n1.astype(BF)], axis=1)], axis=0)
            inter = _dot(pp["q"], rhs_c)
            na = jnp.concatenate([pp["w2"], pp["w2"]], axis=1) * inter + pp["intra"]
            out = na[:, :2 * ML_V] / jnp.maximum(jnp.abs(na[:, 2 * ML_V:]), pp["e2"])
            acc_ref[pre["d"], pl.ds(pre["off"], L), 2 * p * ML_V:(2 * p + 2) * ML_V] = out
            for hh, (c, n) in enumerate(((c0, n0), (c1, n1))):
                h = 2 * p + hh
                f_old = pre["f_v"][0:1, h * ML_V:(h + 1) * ML_V]
                f_loc = pre["f_v"][1:2, h * ML_V:(h + 1) * ML_V]
                rs = slice(hh * ML_QK, (hh + 1) * ML_QK)
                new_heads.append((f_old * c + f_loc * pp["upd"][rs, hh * ML_V:(hh + 1) * ML_V],
                                  f_old * n + f_loc * pp["upd"][rs, 2 * ML_V:3 * ML_V]))
        return tuple(new_heads)

    hn = hn_ref[...]
    rows = _ML_GROUP * L

    def finish_rows(row0):
        hb = acc_ref[0, pl.ds(row0, rows), :] + acc_ref[1, pl.ds(row0, rows), :]
        og = og_ref[pl.ds(row0, rows), :]
        yield
        gate = jax.nn.sigmoid(og.astype(F32))
        for h in range(ML_HEADS):
            sl = slice(h * ML_V, (h + 1) * ML_V)
            x = hb[:, sl]
            y = x * lax.rsqrt(jnp.mean(x * x, axis=-1, keepdims=True) + EPS) * hn[:, sl]
            y_ref[pl.ds(row0, rows), sl] = (y * gate[:, sl]).astype(y_ref.dtype)
            yield

    def group(j, carry, finish):
        (heads_f, m_f), (heads_b, m_b) = carry
        chunks = [(j * _ML_GROUP + k, nc - 1 - (j * _ML_GROUP + k)) for k in range(_ML_GROUP)]
        ms_f, ms_b = [m_f], [m_b]
        for cf, cb in chunks:
            ms_f.append(jnp.maximum(st_ref[cf][0:1, :] + ms_f[-1], st_ref[cf][1:2, :]))
            ms_b.append(jnp.maximum(st_ref[cb][0:1, :] + ms_b[-1], st_ref[cb][1:2, :]))
        gens = ([chunk_pre(cf, 0, ms_f[k]) for k, (cf, _) in enumerate(chunks)]
                + [chunk_pre(cb, 1, ms_b[k]) for k, (_, cb) in enumerate(chunks)])
        if finish:
            gens.append(finish_rows(pl.multiple_of((nc - _ML_GROUP * j) * L, rows)))
            gens.append(finish_rows(pl.multiple_of(_ML_GROUP * (j - 1) * L, rows)))
        pres = _run_interleaved(gens)
        for k in range(_ML_GROUP):
            heads_f = chunk_post(pres[k], heads_f)
            heads_b = chunk_post(pres[_ML_GROUP + k], heads_b)
        return (heads_f, ms_f[-1]), (heads_b, ms_b[-1])

    zero_state = jnp.zeros((ML_QK, ML_V), F32)
    init = (tuple((zero_state, zero_state) for _ in range(ML_HEADS)), jnp.zeros((1, LANES), F32))
    n_groups = nc // _ML_GROUP
    crossed = n_groups // 2 + 1
    carry = lax.fori_loop(0, crossed, functools.partial(group, finish=False), (init, init))
    lax.fori_loop(crossed, n_groups, functools.partial(group, finish=True), carry)
    _run_interleaved([finish_rows(0), finish_rows(t - rows)])


def _mlstm(qk, v, og, g_col, g_row, hn):
    b, t, _ = qk.shape
    nc = t // ML_CHUNK
    seq = lambda i: (i, 0, 0)
    return pl.pallas_call(
        _mlstm_kernel,
        out_shape=jax.ShapeDtypeStruct((b, t, ML_V_W), BF),
        grid=(b,),
        in_specs=[
            pl.BlockSpec((None, t, 2 * ML_QK_W), seq),
            pl.BlockSpec((None, t, ML_V_W), seq),
            pl.BlockSpec((None, t, ML_V_W), seq),
            pl.BlockSpec((None, t, LANES), seq),
            pl.BlockSpec((None, nc, PAIR_ROWS, LANES), lambda i: (i, 0, 0, 0)),
            pl.BlockSpec((1, ML_V_W), lambda i: (0, 0)),
        ],
        out_specs=pl.BlockSpec((None, t, ML_V_W), seq),
        scratch_shapes=[
            pltpu.VMEM((2, t, ML_V_W), F32),
            pltpu.VMEM((t, LANES), F32),
            pltpu.VMEM((t, LANES), F32),
            pltpu.VMEM((t, 2 * ML_QK_W), BF),
            pltpu.VMEM((nc, 8, LANES), F32),
        ],
        compiler_params=pltpu.CompilerParams(
            dimension_semantics=("parallel",), vmem_limit_bytes=VMEM_LIMIT),
    )(qk, v, og, g_col, g_row, hn)


_GD_QKV_W = 2 * GD_K_W + GD_V_W
_GD_PAIRS = GD_HEADS // 2
_GD_PD = 2 * _GD_PAIRS
_GD_ROW_A = 8
_GD_A_CHUNKS = 4


def _gdn_kernel(qkv_ref, z_ref, gc_ref, gr_ref, acol_ref, arow_ref, hn_ref, y_ref,
                acc_ref, u_ref, wq_ref, att_ref, kd_ref, egl_ref):
    t = qkv_ref.shape[0]
    L = GD_CHUNK
    nc = t // L
    rows = 256
    incl_lo, incl_up, _, _ = _tri_masks(L)
    lower_b = jnp.where(incl_lo, 1.0, 0.0).astype(BF)
    upper_b = jnp.where(incl_up, 1.0, 0.0).astype(BF)

    tt = lax.broadcasted_iota(jnp.int32, (L, LANES), 0)
    ln = lax.broadcasted_iota(jnp.int32, (L, LANES), 1)
    ss = ln % GD_K
    lane_lo = ln < GD_K
    incl = (ss <= tt, ss >= tt)
    strict = (ss < tt, ss > tt)
    eye_pair = jnp.where(ss == tt, 1.0, 0.0)
    bw_cols = (ln // GD_HEADS) % 2 == 1
    r2 = lax.broadcasted_iota(jnp.int32, (2 * L, LANES), 0)
    c2 = lax.broadcasted_iota(jnp.int32, (2 * L, LANES), 1)
    same_half = (r2 // L) == (c2 // GD_K)
    bd_upper = jnp.where(same_half & (c2 % GD_K >= r2 % L), 1.0, 0.0).astype(BF)
    bd_lower = jnp.where(same_half & (c2 % GD_K <= r2 % L), 1.0, 0.0).astype(BF)
    ej = lax.broadcasted_iota(jnp.int32, (LANES, _GD_PD * LANES), 0)
    en = lax.broadcasted_iota(jnp.int32, (LANES, _GD_PD * LANES), 1) // GD_K
    e_a = jnp.where(ej == GATE_A + en, 1.0, 0.0).astype(BF)
    e_b = jnp.where(ej == GATE_B + en, 1.0, 0.0).astype(BF)
    a_col = jnp.exp(acol_ref[...])
    a_row = jnp.exp(arow_ref[...])

    def phase_a(chunk_ids):
        chains = []
        for cidx in chunk_ids:
            off = pl.multiple_of(cidx * L, L)
            gcol = gc_ref[pl.ds(off, L), :]
            grow = gr_ref[cidx]
            qkv = qkv_ref[pl.ds(off, L), :]
            g_c = -a_col * _softplus(gcol)
            cum_c = jnp.where(bw_cols, _sel_dot_rhs(upper_b, g_c), _sel_dot_rhs(lower_b, g_c))
            gcb_all = _sel_dot_lhs(cum_c, e_a, parts=2)
            beb_all = _sel_dot_lhs(jax.nn.sigmoid(gcol), e_b, parts=1)
            g_r = -a_row * _softplus(grow)
            cum_r = (_sel_dot_lhs(g_r, bd_upper), _sel_dot_lhs(g_r, bd_lower))
            for p in range(_GD_PAIRS):
                q_b = qkv[:, p * LANES:(p + 1) * LANES]
                k_b = qkv[:, GD_K_W + p * LANES:GD_K_W + (p + 1) * LANES]
                q_p = q_b.astype(F32)
                k_p = k_b.astype(F32)
                v_p = qkv[:, 2 * GD_K_W + p * LANES:2 * GD_K_W + (p + 1) * LANES].astype(F32)
                kexp = _expand_pair(k_b, lane_lo)
                kk = _dot_nt(k_b, kexp)
                qk = _dot_nt(q_b, kexp)
                for d in range(2):
                    pd = d * _GD_PAIRS + p
                    gcb = gcb_all[:, pd * LANES:(pd + 1) * LANES]
                    beb = beb_all[:, pd * LANES:(pd + 1) * LANES]
                    gcr = cum_r[d][_GD_ROW_A + pd:_GD_ROW_A + pd + 1, :]
                    decay = jnp.exp(jnp.where(incl[d], gcb - gcr, -jnp.inf))
                    lmat = jnp.where(strict[d], beb * kk * decay, 0.0)
                    eg = jnp.exp(gcb)
                    gl = gcb[L - 1:L, :] if d == 0 else gcb[0:1, :]
                    x = jnp.concatenate([v_p * beb, k_p * (beb * eg)], axis=1)
                    att_ref[pd, pl.ds(off, L), :] = (qk * decay).astype(BF)
                    wq_ref[pd, cidx, L:2 * L, :] = (q_p * eg).astype(BF)
                    kd_ref[pd, pl.ds(off, L), :] = (k_p * jnp.exp(gl - gcb)).astype(BF)
                    egl_ref[cidx, pd] = jnp.broadcast_to(jnp.exp(gl), (8, LANES))
                    chains.append(dict(pd=pd, cidx=cidx, off=off, p=lmat.astype(BF), l=lmat, x=x))
            yield

        for ch in chains:
            ch["t"] = eye_pair - ch["l"]
            ch["p"] = _dot(ch["p"], _expand_pair(ch["p"], lane_lo)).astype(BF)
        yield
        power = 2
        while power < L:
            last = 2 * power >= L
            for ch in chains:
                rhs = _expand_pair(ch["t"].astype(BF), lane_lo)
                if not last:
                    rhs = jnp.concatenate([_expand_pair(ch["p"], lane_lo), rhs], axis=1)
                ch["y"] = _dot(ch["p"], rhs)
            yield
            for ch in chains:
                if not last:
                    ch["p"] = ch["y"][:, :LANES].astype(BF)
                ch["t"] = ch["t"] + ch["y"][:, -LANES:]
            power *= 2
        for ch in chains:
            xb = ch["x"].astype(BF)
            rhs = jnp.concatenate([_expand_pair(xb[:, :LANES], lane_lo),
                                   _expand_pair(xb[:, LANES:], lane_lo)], axis=1)
            sol = _dot(ch["t"].astype(BF), rhs)
            u_ref[ch["pd"], pl.ds(ch["off"], L), :] = sol[:, :LANES]
            wq_ref[ch["pd"], ch["cidx"], 0:L, :] = sol[:, LANES:].astype(BF)

    rb = lax.broadcasted_iota(jnp.int32, (LANES, LANES), 0) // GD_K
    cb = lax.broadcasted_iota(jnp.int32, (LANES, LANES), 1) // GD_V
    blockdiag = rb == cb

    def phase_b(steps, states):
        loaded = []
        for i in steps:
            items = []
            for d in range(2):
                cidx = i if d == 0 else nc - 1 - i
                off = pl.multiple_of(cidx * L, L)
                for p in range(_GD_PAIRS):
                    pd = d * _GD_PAIRS + p
                    items.append(dict(
                        d=d, p=p, off=off, wq=wq_ref[pd, cidx], u=u_ref[pd, pl.ds(off, L), :],
                        kd=kd_ref[pd, pl.ds(off, L), :], att=att_ref[pd, pl.ds(off, L), :],
                        egl=egl_ref[cidx, pd][0:1, :]))
            loaded.append(items)
        yield
        for items in loaded:
            s_b = [s.astype(BF) for s in states]
            res = [_dot(it["wq"], s_b[pd]) for pd, it in enumerate(items)]
            yield
            v_new = [(it["u"] - res[pd][:L]).astype(BF) for pd, it in enumerate(items)]
            upd = [_dot_tn(it["kd"], v_new[pd]) for pd, it in enumerate(items)]
            yield
            states = tuple(states[pd] * it["egl"] + jnp.where(blockdiag, upd[pd], 0.0)
                           for pd, it in enumerate(items))
            outs = [res[pd][L:] + _dot(it["att"], _expand_pair(v_new[pd], lane_lo))
                    for pd, it in enumerate(items)]
            yield
            for it, o in zip(items, outs):
                acc_ref[it["d"], pl.ds(it["off"], L), it["p"] * LANES:(it["p"] + 1) * LANES] = o
        return states

    half = _GD_A_CHUNKS // 2
    n_groups = nc // _GD_A_CHUNKS

    def group_chunks(j):
        return [j * half + k for k in range(half)] + [nc - 1 - (j * half + k) for k in range(half)]

    def overlapped(j, states):
        steps = [(j - 1) * half + k for k in range(half)]
        return _run_interleaved([phase_b(steps, states), phase_a(group_chunks(j))])[0]

    hn = hn_ref[...]
    gsum_v = _group_ones(GD_V_W, GD_V)

    def finish_chunks(chunk_ids):
        loaded = []
        for c in chunk_ids:
            off = pl.multiple_of(c * L, L)
            loaded.append((off, acc_ref[0, pl.ds(off, L), :] + acc_ref[1, pl.ds(off, L), :],
                           z_ref[pl.ds(off, L), :]))
        yield
        for off, x, z in loaded:
            z = z.astype(F32)
            ms = _sel_dot_lhs(x * x, gsum_v, parts=2) * (1.0 / GD_V)
            yield
            y = x * lax.rsqrt(ms + EPS) * hn
            y_ref[pl.ds(off, L), :] = (y * (z * jax.nn.sigmoid(z))).astype(y_ref.dtype)
            yield

    def drain(i, states):
        return _run_interleaved([phase_b([i], states)])[0]

    def drain_finish(i, states):
        return _run_interleaved([phase_b([i], states), finish_chunks([i - 1, nc - i])])[0]

    first_finish = nc // 2 + 1
    _run_interleaved([phase_a(group_chunks(0))])
    states = tuple(jnp.zeros((LANES, LANES), F32) for _ in range(_GD_PD))
    states = lax.fori_loop(1, n_groups, overlapped, states)
    states = lax.fori_loop((n_groups - 1) * half, first_finish, drain, states, unroll=True)
    states = lax.fori_loop(first_finish, nc, drain_finish, states, unroll=3)
    _run_interleaved([finish_chunks([0, nc - 1])])


def _gdn(qkv, z, g_col, g_rowp, a_col, a_rowp, hn):
    b, t, _ = qkv.shape
    nc = t // GD_CHUNK
    seq = lambda i: (i, 0, 0)
    const = lambda i: (0, 0)
    return pl.pallas_call(
        _gdn_kernel,
        out_shape=jax.ShapeDtypeStruct((b, t, GD_V_W), BF),
        grid=(b,),
        in_specs=[
            pl.BlockSpec((None, t, _GD_QKV_W), seq),
            pl.BlockSpec((None, t, GD_V_W), seq),
            pl.BlockSpec((None, t, LANES), seq),
            pl.BlockSpec((None, nc, PAIR_ROWS, LANES), lambda i: (i, 0, 0, 0)),
            pl.BlockSpec((1, LANES), const),
            pl.BlockSpec((PAIR_ROWS, LANES), const),
            pl.BlockSpec((1, GD_V_W), const),
        ],
        out_specs=pl.BlockSpec((None, t, GD_V_W), seq),
        scratch_shapes=[
            pltpu.VMEM((2, t, GD_V_W), F32),
            pltpu.VMEM((_GD_PD, t, LANES), F32),
            pltpu.VMEM((_GD_PD, nc, 2 * GD_CHUNK, LANES), BF),
            pltpu.VMEM((_GD_PD, t, LANES), BF),
            pltpu.VMEM((_GD_PD, t, LANES), BF),
            pltpu.VMEM((nc, _GD_PD, 8, LANES), F32),
        ],
        compiler_params=pltpu.CompilerParams(
            dimension_semantics=("parallel",), vmem_limit_bytes=VMEM_LIMIT),
    )(qkv, z, g_col, g_rowp, a_col, a_rowp, hn)


def _outproj_kernel(x_ref, ya_ref, yb_ref, yc_ref, w_ref, o_ref):
    acc = x_ref[...]
    acc = acc + _dot(ya_ref[...], w_ref[0:GM_WIDTH, :])
    acc = acc + _dot(yb_ref[...], w_ref[GM_WIDTH:GM_WIDTH + ML_V_W, :])
    acc = acc + _dot(yc_ref[...], w_ref[GM_WIDTH + ML_V_W:, :])
    o_ref[...] = acc


def _outproj(x2, ya, yb, yc, w, layer, tm):
    n = x2.shape[0]
    row = lambda i: (i, 0)
    return pl.pallas_call(
        _outproj_kernel,
        out_shape=jax.ShapeDtypeStruct((n, D_MODEL), F32),
        grid=(n // tm,),
        in_specs=[
            pl.BlockSpec((tm, D_MODEL), row),
            pl.BlockSpec((tm, GM_WIDTH), row),
            pl.BlockSpec((tm, ML_V_W), row),
            pl.BlockSpec((tm, GD_V_W), row),
            _layer_spec((D_MODEL, D_MODEL), layer),
        ],
        out_specs=pl.BlockSpec((tm, D_MODEL), row),
        compiler_params=pltpu.CompilerParams(
            dimension_semantics=("parallel",), vmem_limit_bytes=VMEM_LIMIT),
    )(x2, ya, yb, yc, w)


_FFN_HALO = 16
_FFN_CHUNK = 256
_FFN_DOWN_GROUP = 4
_FFN_SLOTS = 3


def _ffn_kernel(xm_ref, xp_ref, xn_ref, g_ref, wup_ref, cw_ref, cb_ref, wdn_ref, gf_ref, o_ref,
                h_ref, act_ref, up_ref, *, tiles_per_seq, final_norm):
    i = pl.program_id(0)
    tm = xm_ref.shape[0]
    g = g_ref[...]
    pos = i % tiles_per_seq

    def norm(x):
        return x * lax.rsqrt(jnp.mean(x * x, axis=-1, keepdims=True) + EPS) * g

    hp = jnp.where(pos == 0, 0.0, norm(xp_ref[...]))
    hn = jnp.where(pos == tiles_per_seq - 1, 0.0, norm(xn_ref[...]))
    xm = xm_ref[...]
    h_ref[0:_FFN_HALO, :] = hp.astype(BF)
    h_ref[_FFN_HALO:_FFN_HALO + tm, :] = norm(xm).astype(BF)
    h_ref[_FFN_HALO + tm:, :] = hn.astype(BF)
    hh = h_ref[...]
    cw = cw_ref[...]
    cb = cb_ref[...]
    o_ref[...] = xm

    def up_project(j):
        c0 = j * _FFN_CHUNK
        slot = j % _FFN_SLOTS
        up_ref[slot, :, 0:_FFN_CHUNK] = _dot(hh, wup_ref[:, c0:c0 + _FFN_CHUNK])
        up_ref[slot, :, _FFN_CHUNK:] = _dot(hh, wup_ref[:, D_FF + c0:D_FF + c0 + _FFN_CHUNK])

    def conv(slot, lanes, c0):
        sl = slice(c0, c0 + _FFN_CHUNK)
        return (cw[0:1, sl] * up_ref[slot, _FFN_HALO - 1:_FFN_HALO - 1 + tm, lanes]
                + cw[1:2, sl] * up_ref[slot, _FFN_HALO:_FFN_HALO + tm, lanes]
                + cw[2:3, sl] * up_ref[slot, _FFN_HALO + 1:_FFN_HALO + 1 + tm, lanes] + cb[:, sl])

    n_chunks = D_FF // _FFN_CHUNK
    for j in range(_FFN_SLOTS - 1):
        up_project(j)
    for j in range(n_chunks):
        if j + _FFN_SLOTS - 1 < n_chunks:
            up_project(j + _FFN_SLOTS - 1)
        c0 = j * _FFN_CHUNK
        gt = conv(j % _FFN_SLOTS, slice(0, _FFN_CHUNK), c0)
        vl = conv(j % _FFN_SLOTS, slice(_FFN_CHUNK, 2 * _FFN_CHUNK), D_FF + c0)
        act_ref[:, c0:c0 + _FFN_CHUNK] = (gt * jax.nn.sigmoid(gt) * vl).astype(BF)
        if (j + 1) % _FFN_DOWN_GROUP == 0 or j + 1 == n_chunks:
            k0 = (j // _FFN_DOWN_GROUP) * _FFN_DOWN_GROUP * _FFN_CHUNK
            k1 = c0 + _FFN_CHUNK
            o_ref[...] += _dot(act_ref[:, k0:k1], wdn_ref[k0:k1, :])

    if final_norm:
        y = o_ref[...]
        o_ref[...] = y * lax.rsqrt(jnp.mean(y * y, axis=-1, keepdims=True) + EPS) * gf_ref[...]


def _ffn(x2, gain, w_up, conv_w, conv_b, w_down, gain_final, layer, seq_len, tm, final_norm):
    n = x2.shape[0]
    hb = tm // _FFN_HALO
    nhb = n // _FFN_HALO
    row = lambda i: (i, 0)
    const = lambda i: (0, 0)
    kern = functools.partial(_ffn_kernel, tiles_per_seq=seq_len // tm, final_norm=final_norm)
    return pl.pallas_call(
        kern,
        out_shape=jax.ShapeDtypeStruct((n, D_MODEL), F32),
        grid=(n // tm,),
        in_specs=[
            pl.BlockSpec((tm, D_MODEL), row),
            pl.BlockSpec((_FFN_HALO, D_MODEL), lambda i: (jnp.maximum(i * hb - 1, 0), 0)),
            pl.BlockSpec((_FFN_HALO, D_MODEL), lambda i: (jnp.minimum((i + 1) * hb, nhb - 1), 0)),
            pl.BlockSpec((1, D_MODEL), const),
            _layer_spec((D_MODEL, 2 * D_FF), layer, pipeline_mode=pl.Buffered(1)),
            pl.BlockSpec((FFN_CONV, 2 * D_FF), const),
            pl.BlockSpec((1, 2 * D_FF), const),
            _layer_spec((D_FF, D_MODEL), layer, pipeline_mode=pl.Buffered(1)),
            pl.BlockSpec((1, D_MODEL), const),
        ],
        out_specs=pl.BlockSpec((tm, D_MODEL), row),
        scratch_shapes=[
            pltpu.VMEM((tm + 2 * _FFN_HALO, D_MODEL), BF),
            pltpu.VMEM((tm, D_FF), BF),
            pltpu.VMEM((_FFN_SLOTS, tm + 2 * _FFN_HALO, 2 * _FFN_CHUNK), F32),
        ],
        compiler_params=pltpu.CompilerParams(
            dimension_semantics=("parallel",), vmem_limit_bytes=VMEM_LIMIT),
    )(x2, x2, x2, gain, w_up, conv_w, conv_b, w_down, gain_final)


_TM = 1024
_TM_FFN = 1024


def _stacked_matmul_weights(w_in, w_out, w_up, w_down):
    depth = w_in.shape[0]
    g0 = 2 * GM_WIDTH + 2 * ML_QK_W + 2 * ML_V_W
    g1 = g0 + 4 * ML_HEADS
    c1 = g1 + _GD_QKV_W + GD_V_W
    q0 = 2 * GM_WIDTH
    col_scale = jnp.ones((_IN_MAIN,), F32).at[q0:q0 + ML_QK_W].set(ML_QK ** -0.5)
    w_main = (jnp.concatenate([w_in[:, :, :g0], w_in[:, :, g1:c1]], axis=2) * col_scale).astype(BF)
    w_gate = jnp.concatenate(
        [w_in[:, :, g0:g1], w_in[:, :, c1:], jnp.zeros((depth, D_MODEL, LANES - N_GATES), F32)],
        axis=2).astype(BF)
    return w_main, w_gate, w_out.astype(BF), w_up.astype(BF), w_down.astype(BF)


def _layer(x2, batch, seq_len, layer, big, norm_mix, gm_norm, gm_ws, gm_bs, ml_gate_bias, ml_head_norm,
           gd_conv, gd_a_log, gd_dt_bias, gd_head_norm, norm_ffn, ffn_conv, ffn_conv_b,
           norm_final, final_norm):
    n = x2.shape[0]
    nc = seq_len // GD_CHUNK
    w_main, w_gate, w_out, w_up, w_down = big
    gate_bias = jnp.concatenate(
        [ml_gate_bias, gd_dt_bias.reshape(-1), jnp.zeros((LANES - GATE_B,), F32)]).reshape(1, LANES)
    uv, qk, v, og, cq, z, gates = _inproj(
        x2, norm_mix.reshape(1, D_MODEL), w_main, w_gate, gate_bias, gd_conv, layer, seq_len, _TM)

    seq = lambda a: a.reshape(batch, seq_len, a.shape[-1])
    g_col = seq(gates)
    g32 = g_col[:, :, :N_GATES]
    g_rowp = g32.reshape(batch, nc, GD_CHUNK, 4, 2, _GD_PAIRS, 2).transpose(0, 1, 3, 4, 5, 6, 2)
    g_rowp = g_rowp.reshape(batch, nc, PAIR_ROWS, LANES)

    ws_cat = gm_ws.transpose(1, 0, 2).reshape(GM_CHUNK, GM_GROUPS * GM_CHUNK).astype(BF)
    bs_exp = jnp.repeat(gm_bs.T, GM_DIM, axis=1)
    ya = _gmlp(seq(uv), gm_norm.reshape(1, GM_WIDTH), ws_cat, bs_exp)

    yb = _mlstm(seq(qk), seq(v), seq(og), g_col, g_rowp, ml_head_norm.reshape(1, ML_V_W))

    a_flat = gd_a_log.reshape(-1)
    a_col = jnp.zeros((LANES,), F32).at[GATE_A:GATE_B].set(a_flat).reshape(1, LANES)
    a_rowp = jnp.zeros((PAIR_ROWS, LANES), F32).at[_GD_ROW_A:_GD_ROW_A + _GD_PD].set(
        jnp.repeat(a_flat.reshape(_GD_PD, 2), GD_K, axis=1))
    yc = _gdn(seq(cq), seq(z), g_col, g_rowp, a_col, a_rowp, gd_head_norm.reshape(1, GD_V_W))

    flat = lambda a: a.reshape(n, a.shape[-1])
    x2 = _outproj(x2, flat(ya), flat(yb), flat(yc), w_out, layer, _TM)
    return _ffn(x2, norm_ffn.reshape(1, D_MODEL), w_up, ffn_conv, ffn_conv_b.reshape(1, -1),
                w_down, norm_final.reshape(1, D_MODEL), layer, seq_len, _TM_FFN, final_norm)


def kernel(x, norm_mix, w_in, gm_norm, gm_ws, gm_bs, ml_gate_bias, ml_head_norm, gd_conv, gd_A_log,
           gd_dt_bias, gd_head_norm, w_out, norm_ffn, w_up, ffn_conv, ffn_conv_b, w_down, norm_final):
    batch, seq_len, _ = x.shape
    depth = w_in.shape[0]
    x2 = x.reshape(batch * seq_len, D_MODEL)
    big = _stacked_matmul_weights(w_in, w_out, w_up, w_down)
    for i in range(depth):
        x2 = _layer(x2, batch, seq_len, i, big, norm_mix[i], gm_norm[i], gm_ws[i], gm_bs[i],
                    ml_gate_bias[i], ml_head_norm[i], gd_conv[i], gd_A_log[i], gd_dt_bias[i],
                    gd_head_norm[i], norm_ffn[i], ffn_conv[i], ffn_conv_b[i],
                    norm_final, i == depth - 1)
    return x2.reshape(batch, seq_len, D_MODEL)
```
